```python
import jax
import jax.numpy as jnp
from jax import lax
import numpy as np

D_MODEL = 2048
BATCH = 4
SEQ = 8192
DEPTH = 2
DEC_BATCH = 32
DEC_SEQ = 64
PAST_LEN = 4096

CHUNK = 64
Q_BLOCK = 128
A_HEADS = 8
A_KV_HEADS = 2
A_HEAD_DIM = 128
A_WIDTH = A_HEADS * A_HEAD_DIM
A_KV_WIDTH = A_KV_HEADS * A_HEAD_DIM
IDX_HEADS = 16
IDX_DIM = 64
TOPK_MAX = 256
B_HEADS = 16
B_HEAD_DIM = 64
B_WIDTH = B_HEADS * B_HEAD_DIM
DECAY_LORA = 64
ICL_LORA = 64
GATE_LORA = 160
GN_EPS = 64e-5
MIX_WIDTH = A_WIDTH + B_WIDTH
N_EXPERTS = 16
N_GROUPS = 4
EXPERTS_PER_GROUP = 4
TOP_K = 2
D_EXPERT = 1024
LN_EPS = 1e-5
ALPHA = (2 * DEPTH) ** 0.25
BETA = (8 * DEPTH) ** -0.25
A_COLS = (A_WIDTH, A_KV_WIDTH, A_KV_WIDTH, IDX_HEADS * IDX_DIM, IDX_DIM, IDX_HEADS)
B_COLS = (B_WIDTH, B_WIDTH, B_WIDTH, DECAY_LORA, ICL_LORA, GATE_LORA)
A_PROJ = A_WIDTH + 2 * A_KV_WIDTH + IDX_HEADS * IDX_DIM + IDX_DIM + IDX_HEADS
B_PROJ = 3 * B_WIDTH + DECAY_LORA + ICL_LORA + GATE_LORA
IN_PROJ = A_PROJ + B_PROJ

kernel_name = 'hymba_dsa_rwkv7_moe_stream_step'


def split_cols(h, sizes):
    offs = [int(o) for o in np.cumsum(sizes)[:-1]]
    return jnp.split(h, offs, axis=-1)


def layer_norm(x, g, b):
    xf = x.astype(jnp.float32)
    mu = xf.mean(-1, keepdims=True)
    var = jnp.square(xf - mu).mean(-1, keepdims=True)
    return ((xf - mu) * lax.rsqrt(var + LN_EPS) * g + b).astype(x.dtype)


def alibi_slopes():
    return 2.0 ** (-8.0 * (jnp.arange(A_HEADS, dtype=jnp.float32) + 1.0) / A_HEADS)


def dsa_attend(q, qi, wi, q_pos, k, v, ki, k_sel):
    Bt, Tq = q.shape[:2]
    L = k.shape[1]
    limit = jnp.minimum((q_pos // CHUNK + 1) * CHUNK, L)
    key_pos = jnp.arange(L, dtype=jnp.int32)
    dots = jnp.einsum('bthd,bsd->bths', qi, ki)
    score = jnp.einsum('bth,bths->bts', wi, jax.nn.relu(dots)).astype(jnp.float32)
    admissible = key_pos[None, :] < limit[:, None]
    score = jnp.where(admissible[None], score, -jnp.inf)
    _, idx = lax.top_k(score, k_sel)
    valid = idx < limit[None, :, None]
    kg = jax.vmap(lambda kb, ib: kb[ib])(k, idx)
    vg = jax.vmap(lambda vb, ib: vb[ib])(v, idx)
    grp = A_HEADS // A_KV_HEADS
    qg = q.reshape(Bt, Tq, A_KV_HEADS, grp, A_HEAD_DIM)
    logits = jnp.einsum('btngd,btjnd->btngj', qg, kg).astype(jnp.float32) * (A_HEAD_DIM ** -0.5)
    dist = jnp.abs(q_pos[None, :, None] - idx).astype(jnp.float32)
    slopes = alibi_slopes().reshape(A_KV_HEADS, grp)
    logits = logits - slopes[None, None, :, :, None] * dist[:, :, None, None, :]
    logits = jnp.where(valid[:, :, None, None, :], logits, -jnp.inf)
    p = jax.nn.softmax(logits, axis=-1).astype(vg.dtype)
    out = jnp.einsum('btngj,btjnd->btngd', p, vg)
    return out.reshape(Bt, Tq, A_WIDTH)


def dsa_prompt(q, qi, wi, k, v, ki, k_sel):
    Bt, T = q.shape[:2]
    nb = T // Q_BLOCK

    def to_blocks(a):
        return jnp.moveaxis(a.reshape((Bt, nb, Q_BLOCK) + a.shape[2:]), 1, 0)

    pos = jnp.arange(T, dtype=jnp.int32).reshape(nb, Q_BLOCK)
    out = lax.map(lambda a: dsa_attend(a[0], a[1], a[2], a[3], k, v, ki, k_sel),
                  (to_blocks(q), to_blocks(qi), to_blocks(wi), pos))
    return jnp.moveaxis(out, 0, 1).reshape(Bt, T, A_WIDTH)


def rwkv_mix(pB, prev_row, S0, mu, w0, w2, a0, a2, g2, k_k, k_a, r_k, gn_g, gn_b):
    f32 = jnp.float32
    Bt, T = pB.shape[:2]
    p_prev = jnp.concatenate([prev_row.astype(pB.dtype), pB[:, :-1]], axis=1)
    ps = pB + (p_prev - pB) * mu
    r, k, v, wl, al, gl = split_cols(ps, B_COLS)
    w = -jax.nn.softplus(-(w0 + jnp.tanh(wl) @ w2).astype(f32)) - 0.5
    decay = jnp.exp(-jnp.exp(w))
    a = jax.nn.sigmoid((a0 + al @ a2).astype(f32))
    g = jax.nn.sigmoid(gl) @ g2

    def hs(t):
        return t.reshape(Bt, T, B_HEADS, B_HEAD_DIM).astype(f32)

    kk = hs(k * k_k)
    kk = kk / jnp.maximum(jnp.sqrt(jnp.sum(kk * kk, axis=-1, keepdims=True)), 1e-12)
    k_mod = hs(k) * (1.0 + (hs(a) - 1.0) * k_a.reshape(B_HEADS, B_HEAD_DIM).astype(f32))
    r4, v4, a4, d4 = hs(r), hs(v), hs(a), hs(decay)

    def step(S, inp):
        r_t, d_t, k_t, v_t, kk_t, a_t = inp
        sa = jnp.einsum('bhij,bhj->bhi', S, -kk_t)
        S = (S * d_t[:, :, None, :] + sa[..., None] * (kk_t * a_t)[:, :, None, :]
             + v_t[..., None] * k_t[:, :, None, :])
        return S, jnp.einsum('bhij,bhj->bhi', S, r_t)

    tm = lambda t: jnp.moveaxis(t, 1, 0)
    S_T, o = lax.scan(step, S0.astype(f32), (tm(r4), tm(d4), tm(k_mod), tm(v4), tm(kk), tm(a4)))
    o = jnp.moveaxis(o, 0, 1)
    om = o.mean(-1, keepdims=True)
    ov = jnp.square(o - om).mean(-1, keepdims=True)
    o = ((o - om) * lax.rsqrt(ov + GN_EPS)).reshape(Bt, T, B_WIDTH) * gn_g + gn_b
    bonus = jnp.sum(r4 * k_mod * r_k.astype(f32), axis=-1, keepdims=True) * v4
    o = (o + bonus.reshape(Bt, T, B_WIDTH)) * g
    return o.astype(pB.dtype), S_T.astype(S0.dtype), pB[:, -1:]


def moe(x, router_w, router_b, w1, w3, w2):
    Bt, T, D = x.shape
    xt = x.reshape(-1, D)
    scores = jax.nn.sigmoid((xt @ router_w).astype(jnp.float32))
    biased = (scores + router_b.astype(jnp.float32)).reshape(-1, N_GROUPS, EXPERTS_PER_GROUP)
    group_score = lax.top_k(biased, TOP_K)[0].sum(-1)
    g_sel = jnp.argmax(group_score, axis=-1)
    in_group = jnp.take_along_axis(biased, g_sel[:, None, None], axis=1)[:, 0]
    _, local = lax.top_k(in_group, TOP_K)
    e_idx = g_sel[:, None] * EXPERTS_PER_GROUP + local
    wsel = jnp.take_along_axis(scores, e_idx, axis=1)
    wsel = wsel / wsel.sum(-1, keepdims=True)
    gates = jnp.einsum('nk,nke->ne', wsel, jax.nn.one_hot(e_idx, N_EXPERTS, dtype=jnp.float32)).astype(x.dtype)
    y = jnp.zeros_like(xt)
    for e in range(N_EXPERTS):
        h = jax.nn.silu(xt @ w1[e]) * (xt @ w3[e])
        y = y + gates[:, e:e + 1] * (h @ w2[e])
    return y.reshape(Bt, T, D)


def trunk_layer(x, l, prm, past):
    Bt, T, _ = x.shape
    h = x @ prm['w_in'][l]
    hA, hB = h[..., :A_PROJ], h[..., A_PROJ:]
    q, k, v, qi, ki, wi = split_cols(hA, A_COLS)
    q = q.reshape(Bt, T, A_HEADS, A_HEAD_DIM)
    k = k.reshape(Bt, T, A_KV_HEADS, A_HEAD_DIM)
    v = v.reshape(Bt, T, A_KV_HEADS, A_HEAD_DIM)
    qi = qi.reshape(Bt, T, IDX_HEADS, IDX_DIM)
    wi = wi * ((IDX_HEADS * IDX_DIM) ** -0.5)
    if past is None:
        oA = dsa_prompt(q, qi, wi, k, v, ki, min(TOPK_MAX, T // 4))
        S0 = jnp.zeros((Bt, B_HEADS, B_HEAD_DIM, B_HEAD_DIM), jnp.float32)
        prev = jnp.zeros((Bt, 1, B_PROJ), x.dtype)
    else:
        ck, cv, cki, S0, prev = past
        kf = jnp.concatenate([ck.astype(k.dtype), k], axis=1)
        vf = jnp.concatenate([cv.astype(v.dtype), v], axis=1)
        kif = jnp.concatenate([cki.astype(ki.dtype), ki], axis=1)
        L = kf.shape[1]
        q_pos = ck.shape[1] + jnp.arange(T, dtype=jnp.int32)
        oA = dsa_attend(q, qi, wi, q_pos, kf, vf, kif, min(TOPK_MAX, L // 4))
    oB, S_T, last = rwkv_mix(hB, prev, S0, prm['rwkv_mu'][l], prm['rwkv_w0'][l], prm['rwkv_w2'][l],
                             prm['rwkv_a0'][l], prm['rwkv_a2'][l], prm['rwkv_g2'][l], prm['rwkv_kk'][l],
                             prm['rwkv_ka'][l], prm['rwkv_rk'][l], prm['rwkv_gn_g'][l], prm['rwkv_gn_b'][l])
    mix = jnp.concatenate([oA, oB], axis=-1) @ prm['w_out'][l]
    x = layer_norm(ALPHA * x + mix, prm['ln1_g'][l], prm['ln1_b'][l])
    f = moe(x, prm['router_w'], prm['router_b'], prm['exp_w1'][l], prm['exp_w3'][l], prm['exp_w2'][l])
    x = layer_norm(ALPHA * x + f, prm['ln2_g'][l], prm['ln2_b'][l])
    return x, (k, v, ki, S_T, last)


def setup_inputs(seed: int = 0) -> dict:
    key = jax.random.key(seed)
    ks = jax.random.split(key, 32)
    f32 = jnp.float32

    def nrm(k, shape, s):
        return jax.random.normal(k, shape, f32) * s

    col_scale = np.concatenate([np.full(n, s, np.float32) for n, s in zip(
        A_COLS + B_COLS, (1.0, 1.0, BETA, 1.0, 1.0, 1.0, 1.0, 1.0, BETA, 1.0, 1.0, 1.0))])
    return {
        'x_prompt': nrm(ks[0], (BATCH, SEQ, D_MODEL), 1.0),
        'x_sample': nrm(ks[1], (DEC_BATCH, DEC_SEQ, D_MODEL), 1.0),
        'cache_k': nrm(ks[2], (DEPTH, DEC_BATCH, PAST_LEN, A_KV_HEADS, A_HEAD_DIM), 1.0),
        'cache_v': nrm(ks[3], (DEPTH, DEC_BATCH, PAST_LEN, A_KV_HEADS, A_HEAD_DIM), BETA),
        'cache_kidx': nrm(ks[4], (DEPTH, DEC_BATCH, PAST_LEN, IDX_DIM), 1.0),
        'state_wkv': nrm(ks[5], (DEPTH, DEC_BATCH, B_HEADS, B_HEAD_DIM, B_HEAD_DIM), 0.5),
        'state_shift': nrm(ks[6], (DEPTH, DEC_BATCH, 1, B_PROJ), 1.0),
        'w_in': nrm(ks[7], (DEPTH, D_MODEL, IN_PROJ), D_MODEL ** -0.5) * jnp.asarray(col_scale),
        'w_out': nrm(ks[8], (DEPTH, MIX_WIDTH, D_MODEL), BETA * MIX_WIDTH ** -0.5),
        'ln1_g': 1.0 + nrm(ks[9], (DEPTH, D_MODEL), 0.02),
        'ln1_b': nrm(ks[10], (DEPTH, D_MODEL), 0.02),
        'ln2_g': 1.0 + nrm(ks[11], (DEPTH, D_MODEL), 0.02),
        'ln2_b': nrm(ks[12], (DEPTH, D_MODEL), 0.02),
        'rwkv_mu': jax.random.uniform(ks[13], (DEPTH, B_PROJ), f32, 0.0, 1.0),
        'rwkv_w0': jax.random.uniform(ks[14], (DEPTH, B_WIDTH), f32, -6.5, -0.5),
        'rwkv_w2': nrm(ks[15], (DEPTH, DECAY_LORA, B_WIDTH), DECAY_LORA ** -0.5),
        'rwkv_a0': nrm(ks[16], (DEPTH, B_WIDTH), 0.1),
        'rwkv_a2': nrm(ks[17], (DEPTH, ICL_LORA, B_WIDTH), ICL_LORA ** -0.5),
        'rwkv_g2': nrm(ks[18], (DEPTH, GATE_LORA, B_WIDTH), GATE_LORA ** -0.5),
        'rwkv_kk': 0.85 + nrm(ks[19], (DEPTH, B_WIDTH), 0.05),
        'rwkv_ka': 1.0 + nrm(ks[20], (DEPTH, B_WIDTH), 0.05),
        'rwkv_rk': nrm(ks[21], (DEPTH, B_HEADS, B_HEAD_DIM), 0.1),
        'rwkv_gn_g': 1.0 + nrm(ks[22], (DEPTH, B_WIDTH), 0.02),
        'rwkv_gn_b': nrm(ks[23], (DEPTH, B_WIDTH), 0.02),
        'router_w': nrm(ks[24], (D_MODEL, N_EXPERTS), D_MODEL ** -0.5),
        'router_b': nrm(ks[25], (N_EXPERTS,), 0.01),
        'exp_w1': nrm(ks[26], (DEPTH, N_EXPERTS, D_MODEL, D_EXPERT), D_MODEL ** -0.5),
        'exp_w3': nrm(ks[27], (DEPTH, N_EXPERTS, D_MODEL, D_EXPERT), D_MODEL ** -0.5),
        'exp_w2': nrm(ks[28], (DEPTH, N_EXPERTS, D_EXPERT, D_MODEL), BETA * D_EXPERT ** -0.5),
    }


def reference(x_prompt, x_sample, cache_k, cache_v, cache_kidx, state_wkv, state_shift,
              w_in, w_out, ln1_g, ln1_b, ln2_g, ln2_b, rwkv_mu, rwkv_w0, rwkv_w2, rwkv_a0,
              rwkv_a2, rwkv_g2, rwkv_kk, rwkv_ka, rwkv_rk, rwkv_gn_g, rwkv_gn_b,
              router_w, router_b, exp_w1, exp_w3, exp_w2):
    prm = dict(w_in=w_in, w_out=w_out, ln1_g=ln1_g, ln1_b=ln1_b, ln2_g=ln2_g, ln2_b=ln2_b,
               rwkv_mu=rwkv_mu, rwkv_w0=rwkv_w0, rwkv_w2=rwkv_w2, rwkv_a0=rwkv_a0, rwkv_a2=rwkv_a2,
               rwkv_g2=rwkv_g2, rwkv_kk=rwkv_kk, rwkv_ka=rwkv_ka, rwkv_rk=rwkv_rk,
               rwkv_gn_g=rwkv_gn_g, rwkv_gn_b=rwkv_gn_b, router_w=router_w, router_b=router_b,
               exp_w1=exp_w1, exp_w3=exp_w3, exp_w2=exp_w2)
    yp = x_prompt
    ys = x_sample
    new_p = []
    new_s = []
    for l in range(DEPTH):
        yp, st_p = trunk_layer(yp, l, prm, None)
        ys, st_s = trunk_layer(ys, l, prm, (cache_k[l], cache_v[l], cache_kidx[l], state_wkv[l], state_shift[l]))
        new_p.append(st_p)
        new_s.append(st_s)
    k_prompt = jnp.stack([s[0] for s in new_p])
    v_prompt = jnp.stack([s[1] for s in new_p])
    kidx_prompt = jnp.stack([s[2] for s in new_p])
    wkv_prompt = jnp.stack([s[3] for s in new_p])
    shift_prompt = jnp.stack([s[4] for s in new_p])
    k_sample = jnp.stack([s[0] for s in new_s])
    v_sample = jnp.stack([s[1] for s in new_s])
    kidx_sample = jnp.stack([s[2] for s in new_s])
    wkv_sample = jnp.stack([s[3] for s in new_s])
    shift_sample = jnp.stack([s[4] for s in new_s])
    return (yp, ys, k_prompt, v_prompt, kidx_prompt, wkv_prompt, shift_prompt,
            k_sample, v_sample, kidx_sample, wkv_sample, shift_sample)
```

```python
import functools

import numpy as np
import jax
import jax.numpy as jnp
from jax import lax
from jax.experimental import pallas as pl
from jax.experimental.pallas import tpu as pltpu

F32 = jnp.float32
BF16 = jnp.bfloat16

D_MODEL = 2048
CHUNK = 64
Q_BLOCK = 128
A_HEADS = 8
A_KV_HEADS = 2
A_HEAD_DIM = 128
A_WIDTH = A_HEADS * A_HEAD_DIM
A_KV_WIDTH = A_KV_HEADS * A_HEAD_DIM
IDX_HEADS = 16
IDX_DIM = 64
IDX_WIDTH = IDX_HEADS * IDX_DIM
TOPK_MAX = 256
B_HEADS = 16
B_HEAD_DIM = 64
B_WIDTH = B_HEADS * B_HEAD_DIM
DECAY_LORA = 64
ICL_LORA = 64
GATE_LORA = 160
GN_EPS = 64e-5
N_EXPERTS = 16
N_GROUPS = 4
EXPERTS_PER_GROUP = 4
D_EXPERT = 1024
LN_EPS = 1e-5
DEPTH = 2
ALPHA = (2 * DEPTH) ** 0.25
A_PROJ = A_WIDTH + 2 * A_KV_WIDTH + IDX_WIDTH + IDX_DIM + IDX_HEADS
B_PROJ = 3 * B_WIDTH + DECAY_LORA + ICL_LORA + GATE_LORA

LANES = 128
SUBLANES = 8
VMEM_LIMIT = 56 * 1024 * 1024

A1_PAD = A_WIDTH + 2 * A_KV_WIDTH
A1_OFF_K, A1_OFF_V = A_WIDTH, A_WIDTH + A_KV_WIDTH
A2_PAD = IDX_WIDTH + LANES
A2_OFF_KW = IDX_WIDTH
B_PAD = 3 * B_WIDTH + LANES + 2 * LANES
B_OFF_LORA = 3 * B_WIDTH
B_OFF_GATE = B_OFF_LORA + LANES

INT_MIN = np.int32(-2 ** 31)
INT_MAX = np.int32(2 ** 31 - 1)
MASKED_LOGIT = -2e30
INIT_MAX = -1e30


def _cparams(sem):
    return pltpu.CompilerParams(dimension_semantics=sem, vmem_limit_bytes=VMEM_LIMIT)


def _mm_kernel(x_ref, w_ref, o_ref):
    o_ref[...] = jnp.dot(x_ref[...].astype(BF16), w_ref[...], preferred_element_type=F32)


def _split(x):
    hi = x.astype(BF16)
    return hi, (x - hi.astype(F32)).astype(BF16)


def _mm_split_kernel(x_ref, wh_ref, wl_ref, o_ref):
    hi, lo = _split(x_ref[...])
    wh = wh_ref[...]
    o_ref[...] = (jnp.dot(hi, wh, preferred_element_type=F32) + jnp.dot(lo, wh, preferred_element_type=F32)
                  + jnp.dot(hi, wl_ref[...], preferred_element_type=F32))


def _matmul(x, w, tm, tn, w_lo=None):
    M, K = x.shape
    N = w.shape[1]
    wspec = pl.BlockSpec((K, tn), lambda j, i: (0, j))
    ws = (w,) if w_lo is None else (w, w_lo)
    return pl.pallas_call(
        _mm_kernel if w_lo is None else _mm_split_kernel,
        grid=(N // tn, M // tm),
        in_specs=[pl.BlockSpec((tm, K), lambda j, i: (i, 0))] + [wspec] * len(ws),
        out_specs=pl.BlockSpec((tm, tn), lambda j, i: (i, j)),
        out_shape=jax.ShapeDtypeStruct((M, N), F32),
        compiler_params=_cparams(("parallel", "parallel")),
        name="in_proj" if w_lo is None else "in_proj_split",
    )(x, *ws)


def _dsa_kernel(q_ref, qi_ref, kw_ref, k_ref, v_ref, ki_ref, o_ref,
                key_scr, qim_scr, qs_scr, m_scr, l_scr, acc_scr,
                *, TQ, TK, L, pos0, ksel, idx_bits):
    i = pl.program_id(1)
    qpos0 = pos0 + i * TQ
    row = lax.broadcasted_iota(jnp.int32, (TQ, 1), 0)
    qpos = qpos0 + row
    limit = jnp.minimum((qpos // CHUNK + 1) * CHUNK, L)
    limit_max = jnp.minimum(((qpos0 + TQ - 1) // CHUNK + 1) * CHUNK, L)
    nkt = (limit_max + TK - 1) // TK
    lane = lax.broadcasted_iota(jnp.int32, (1, TK), 1)

    kw = kw_ref[0]
    half = lax.broadcasted_iota(jnp.int32, (1, LANES), 1) < IDX_DIM
    for p in range(IDX_HEADS // 2):
        slab = qi_ref[0, :, p * LANES:(p + 1) * LANES]
        hi = slab.astype(BF16).astype(F32)
        lo = slab - hi
        hi_sw = pltpu.roll(hi, IDX_DIM, 1)
        lo_sw = pltpu.roll(lo, IDX_DIM, 1)
        qim_scr[2 * p, :, 0:LANES] = jnp.where(half, hi, lo_sw).astype(BF16)
        qim_scr[2 * p, :, LANES:2 * LANES] = jnp.where(half, hi, 0.0).astype(BF16)
        qim_scr[2 * p + 1, :, 0:LANES] = jnp.where(half, hi_sw, lo).astype(BF16)
        qim_scr[2 * p + 1, :, LANES:2 * LANES] = jnp.where(half, hi_sw, 0.0).astype(BF16)
    wcols = [kw[:, IDX_DIM + h:IDX_DIM + h + 1] * (IDX_WIDTH ** -0.5) for h in range(IDX_HEADS)]

    def score_tile(kt, carry):
        ks = pl.multiple_of(kt * TK, TK)
        ki_t = ki_ref[0, pl.ds(ks, TK), :]
        acc = jnp.zeros((TQ, TK), F32)
        for h in range(IDX_HEADS):
            d = lax.dot_general(qim_scr[h], ki_t, (((1,), (1,)), ((), ())),
                                preferred_element_type=F32)
            acc = acc + jnp.maximum(d, 0.0) * wcols[h]
        bits = pltpu.bitcast(acc, jnp.int32)
        key = bits ^ ((bits >> 31) & INT_MAX)
        key = jnp.where(acc == 0.0, 0, key)
        kpos = ks + lane
        key = jnp.where(kpos < limit, key, INT_MIN)
        key_scr[:, pl.ds(ks, TK)] = key
        return carry

    lax.fori_loop(0, nkt, score_tile, 0)

    def count(pred_fn):
        def body(kt, cnt):
            ks = pl.multiple_of(kt * TK, TK)
            m = pred_fn(key_scr[:, pl.ds(ks, TK)], ks)
            c = jnp.where(m, 1.0, 0.0)
            for s in range(TK // LANES):
                cnt = cnt + c[:, s * LANES:(s + 1) * LANES]
            return cnt
        cnt = lax.fori_loop(0, nkt, body, jnp.zeros((TQ, LANES), F32))
        return jnp.sum(cnt, axis=1, keepdims=True)

    kself = float(ksel)

    def bit_step(b, thr):
        cand = thr + jnp.left_shift(jnp.int32(1), 31 - b)
        c = count(lambda key, ks: key >= cand)
        return jnp.where(c >= kself, cand, thr)

    thr = lax.fori_loop(0, 32, bit_step, jnp.full((TQ, 1), INT_MIN, jnp.int32))
    n_gt = count(lambda key, ks: key > thr)
    n_ge = count(lambda key, ks: key >= thr)
    need = kself - n_gt

    def idx_step(b, jm):
        cand = jm + jnp.left_shift(jnp.int32(1), idx_bits - 1 - b)
        c = count(lambda key, ks: (key == thr) & ((ks + lane) < cand))
        return jnp.where(c < need, cand, jm)

    jm = lax.fori_loop(0, idx_bits, idx_step, jnp.zeros((TQ, 1), jnp.int32))
    jm = jnp.where(n_ge > kself, jm, INT_MAX)

    grp = A_HEADS // A_KV_HEADS
    for n in range(A_KV_HEADS):
        for g in range(grp):
            hh = n * grp + g
            qs_scr[n, g * TQ:(g + 1) * TQ, :] = q_ref[0, :, hh * A_HEAD_DIM:(hh + 1) * A_HEAD_DIM].astype(BF16)
    m_scr[...] = jnp.full(m_scr.shape, INIT_MAX, F32)
    l_scr[...] = jnp.zeros(l_scr.shape, F32)
    acc_scr[...] = jnp.zeros(acc_scr.shape, F32)
    scale = A_HEAD_DIM ** -0.5
    slopes = [float(2.0 ** (-8.0 * (h + 1.0) / A_HEADS)) for h in range(A_HEADS)]

    def attn_tile(kt, carry):
        ks = pl.multiple_of(kt * TK, TK)
        key = key_scr[:, pl.ds(ks, TK)]
        kpos = ks + lane
        sel = ((key > thr) | ((key == thr) & (kpos <= jm))) & (kpos < limit)
        dist = jnp.abs(qpos - kpos).astype(F32)
        for n in range(A_KV_HEADS):
            k_t = k_ref[0, pl.ds(ks, TK), n * A_HEAD_DIM:(n + 1) * A_HEAD_DIM]
            v_t = v_ref[0, pl.ds(ks, TK), n * A_HEAD_DIM:(n + 1) * A_HEAD_DIM]
            s_all = lax.dot_general(qs_scr[n], k_t, (((1,), (1,)), ((), ())),
                                    preferred_element_type=F32)
            for g in range(grp):
                hh = n * grp + g
                lg = s_all[g * TQ:(g + 1) * TQ] * scale - slopes[hh] * dist
                lg = jnp.where(sel, lg, MASKED_LOGIT)
                m_prev = m_scr[hh]
                m_new = jnp.maximum(m_prev, jnp.max(lg, axis=1, keepdims=True))
                p = jnp.exp(lg - m_new)
                alpha = jnp.exp(m_prev - m_new)
                l_scr[hh] = alpha * l_scr[hh] + jnp.sum(p, axis=1, keepdims=True)
                acc_scr[hh] = alpha * acc_scr[hh] + jnp.dot(p.astype(BF16), v_t,
                                                            preferred_element_type=F32)
                m_scr[hh] = m_new
        return carry

    lax.fori_loop(0, nkt, attn_tile, 0)
    for hh in range(A_HEADS):
        o_ref[0, :, hh * A_HEAD_DIM:(hh + 1) * A_HEAD_DIM] = acc_scr[hh] / l_scr[hh]


def _dsa(hA1, hA2, kb, vb, kib, *, TQ, L, pos0, ksel):
    B, T, _ = hA1.shape
    Lp = kb.shape[1]
    TK = min(512, Lp)
    assert Lp % TK == 0 and TK >= ksel and T % TQ == 0
    idx_bits = int(Lp).bit_length()
    kern = functools.partial(_dsa_kernel, TQ=TQ, TK=TK, L=L, pos0=pos0, ksel=ksel, idx_bits=idx_bits)
    return pl.pallas_call(
        kern,
        grid=(B, T // TQ),
        in_specs=[
            pl.BlockSpec((1, TQ, A_WIDTH), lambda b, i: (b, i, 0)),
            pl.BlockSpec((1, TQ, IDX_WIDTH), lambda b, i: (b, i, 0)),
            pl.BlockSpec((1, TQ, LANES), lambda b, i: (b, i, A2_OFF_KW // LANES)),
            pl.BlockSpec((1, Lp, A_KV_WIDTH), lambda b, i: (b, 0, 0)),
            pl.BlockSpec((1, Lp, A_KV_WIDTH), lambda b, i: (b, 0, 0)),
            pl.BlockSpec((1, Lp, 2 * LANES), lambda b, i: (b, 0, 0)),
        ],
        out_specs=pl.BlockSpec((1, TQ, A_WIDTH), lambda b, i: (b, i, 0)),
        out_shape=jax.ShapeDtypeStruct((B, T, A_WIDTH), F32),
        scratch_shapes=[
            pltpu.VMEM((TQ, Lp), jnp.int32),
            pltpu.VMEM((IDX_HEADS, TQ, 2 * LANES), BF16),
            pltpu.VMEM((A_KV_HEADS, (A_HEADS // A_KV_HEADS) * TQ, A_HEAD_DIM), BF16),
            pltpu.VMEM((A_HEADS, TQ, 1), F32),
            pltpu.VMEM((A_HEADS, TQ, 1), F32),
            pltpu.VMEM((A_HEADS, TQ, A_HEAD_DIM), F32),
        ],
        compiler_params=_cparams(("parallel", "arbitrary")),
        name="dsa",
    )(hA1, hA2, hA2, kb, vb, kib)


def _head_sum(x, bd):
    hi = x.astype(BF16)
    lo = (x - hi.astype(F32)).astype(BF16)
    outs = []
    for s in range(x.shape[1] // LANES):
        sl = slice(s * LANES, (s + 1) * LANES)
        outs.append(jnp.dot(hi[:, sl], bd, preferred_element_type=F32)
                    + jnp.dot(lo[:, sl], bd, preferred_element_type=F32))
    return jnp.concatenate(outs, axis=1)


def _rwkv_prep_kernel(hb_ref, halo_ref, prev_ref, mu_ref, w0_ref, ww_ref, a0_ref, wa_ref, wg_ref,
                      kk_ref, ka_ref, rk_ref, bd_ref,
                      r_out, d_out, k_out, v_out, q_out, b_out, g_out, bonus_out):
    i = pl.program_id(1)
    p = hb_ref[0]
    tb = p.shape[0]
    first_prev = jnp.where(i == 0, prev_ref[0], halo_ref[0, SUBLANES - 1:SUBLANES, :])
    rolled = pltpu.roll(p, 1, 0)
    rowid = lax.broadcasted_iota(jnp.int32, (tb, 1), 0)
    p_prev = jnp.where(rowid == 0, first_prev, rolled)
    ps = p + (p_prev - p) * mu_ref[...]
    r = ps[:, 0:B_WIDTH]
    k = ps[:, B_WIDTH:2 * B_WIDTH]
    v = ps[:, 2 * B_WIDTH:3 * B_WIDTH]
    lora = ps[:, B_OFF_LORA:B_OFF_LORA + LANES]
    gl = ps[:, B_OFF_GATE:B_OFF_GATE + 2 * LANES]
    wz = w0_ref[...] + jnp.dot(jnp.tanh(lora).astype(BF16), ww_ref[...], preferred_element_type=F32)
    w = -jax.nn.softplus(-wz) - 0.5
    decay = jnp.exp(-jnp.exp(w))
    a = jax.nn.sigmoid(a0_ref[...] + jnp.dot(lora.astype(BF16), wa_ref[...], preferred_element_type=F32))
    g = jnp.dot(jax.nn.sigmoid(gl).astype(BF16), wg_ref[...], preferred_element_type=F32)
    bd = bd_ref[...]
    kk = k * kk_ref[...]
    nrm = jnp.maximum(jnp.sqrt(_head_sum(kk * kk, bd)), 1e-12)
    kk = kk / nrm
    k_mod = k * (1.0 + (a - 1.0) * ka_ref[...])
    bonus = _head_sum(r * k_mod * rk_ref[...], bd) * v
    r_out[0] = r
    d_out[0] = decay
    k_out[0] = k_mod
    v_out[0] = v
    q_out[0] = -kk
    b_out[0] = kk * a
    g_out[0] = g
    bonus_out[0] = bonus


def _rwkv_prep(hB, prev, prm, tb):
    B, T, _ = hB.shape
    nh = tb // SUBLANES
    row = lambda n: pl.BlockSpec((1, n), lambda b, i: (0, 0))
    full = lambda a: pl.BlockSpec(a.shape, lambda b, i: (0, 0))
    tok = pl.BlockSpec((1, tb, B_WIDTH), lambda b, i: (b, i, 0))
    return pl.pallas_call(
        _rwkv_prep_kernel,
        grid=(B, T // tb),
        in_specs=[
            pl.BlockSpec((1, tb, B_PAD), lambda b, i: (b, i, 0)),
            pl.BlockSpec((1, SUBLANES, B_PAD), lambda b, i: (b, jnp.maximum(i * nh - 1, 0), 0)),
            pl.BlockSpec((1, 1, B_PAD), lambda b, i: (b, 0, 0)),
            row(B_PAD), row(B_WIDTH), full(prm['ww']), row(B_WIDTH), full(prm['wa']), full(prm['wg']),
            row(B_WIDTH), row(B_WIDTH), row(B_WIDTH), full(prm['bd']),
        ],
        out_specs=[tok] * 8,
        out_shape=[jax.ShapeDtypeStruct((B, T, B_WIDTH), F32)] * 8,
        compiler_params=_cparams(("parallel", "parallel")),
        name="rwkv_prep",
    )(hB, hB, prev, prm['mu'], prm['w0'], prm['ww'], prm['a0'], prm['wa'], prm['wg'],
      prm['kk'], prm['ka'], prm['rk'], prm['bd'])


def _rwkv_scan_kernel(q_ref, d_ref, b_ref, k_ref, r_ref, v_ref, h0_ref, o_ref, hT_ref, h_scr, *, Tc):
    c = pl.program_id(1)
    JH = B_HEAD_DIM // 2

    @pl.when(c == 0)
    def _():
        h_scr[...] = h0_ref[0]

    def token(t, carry):
        u = jnp.zeros((B_HEAD_DIM, LANES), F32)
        for j in range(JH):
            u = u + q_ref[0, t, j:j + 1, :] * h_scr[j]
        u = u + pltpu.roll(u, LANES // 2, 1)
        vt = v_ref[0, t]
        o = jnp.zeros((B_HEAD_DIM, LANES), F32)
        for j in range(JH):
            hn = (h_scr[j] * d_ref[0, t, j:j + 1, :] + u * b_ref[0, t, j:j + 1, :]
                  + vt * k_ref[0, t, j:j + 1, :])
            h_scr[j] = hn
            o = o + hn * r_ref[0, t, j:j + 1, :]
        o_ref[0, t] = o + pltpu.roll(o, LANES // 2, 1)
        return carry

    lax.fori_loop(0, Tc, token, 0)

    @pl.when(c == pl.num_programs(1) - 1)
    def _():
        hT_ref[0] = h_scr[...]


def _rwkv_scan(q, d, b, k, r, v, h0, Tc):
    G, T = q.shape[:2]
    JH = B_HEAD_DIM // 2
    jspec = pl.BlockSpec((1, Tc, JH, LANES), lambda g, c: (g, c, 0, 0))
    ispec = pl.BlockSpec((1, Tc, B_HEAD_DIM, LANES), lambda g, c: (g, c, 0, 0))
    hspec = pl.BlockSpec((1, JH, B_HEAD_DIM, LANES), lambda g, c: (g, 0, 0, 0))
    return pl.pallas_call(
        functools.partial(_rwkv_scan_kernel, Tc=Tc),
        grid=(G, T // Tc),
        in_specs=[jspec] * 5 + [ispec, hspec],
        out_specs=[ispec, hspec],
        out_shape=[jax.ShapeDtypeStruct((G, T, B_HEAD_DIM, LANES), F32),
                   jax.ShapeDtypeStruct((G, JH, B_HEAD_DIM, LANES), F32)],
        scratch_shapes=[pltpu.VMEM((JH, B_HEAD_DIM, LANES), F32)],
        compiler_params=_cparams(("parallel", "arbitrary")),
        name="rwkv_scan",
    )(q, d, b, k, r, v, h0)


def _to_scan_j(x, B, T):
    G = B // 4
    x = x.reshape(G, 4, T, B_HEADS, 2, B_HEAD_DIM // 2)
    return x.transpose(0, 2, 5, 4, 1, 3).reshape(G, T, B_HEAD_DIM // 2, LANES)


def _to_scan_i(x, B, T):
    G = B // 4
    x = x.reshape(G, 4, T, B_HEADS, B_HEAD_DIM).transpose(0, 2, 4, 1, 3).reshape(G, T, B_HEAD_DIM, 64)
    return jnp.concatenate([x, x], axis=-1)


def _from_scan_i(o, B, T):
    G = B // 4
    o = o[..., :64].reshape(G, T, B_HEAD_DIM, 4, B_HEADS).transpose(0, 3, 1, 4, 2)
    return o.reshape(B, T, B_WIDTH)


def _state_to_scan(S, B):
    G = B // 4
    S = S.reshape(G, 4, B_HEADS, B_HEAD_DIM, 2, B_HEAD_DIM // 2)
    return S.transpose(0, 5, 3, 4, 1, 2).reshape(G, B_HEAD_DIM // 2, B_HEAD_DIM, LANES)


def _state_from_scan(Hs, B):
    G = B // 4
    Hs = Hs.reshape(G, B_HEAD_DIM // 2, B_HEAD_DIM, 2, 4, B_HEADS)
    return Hs.transpose(0, 4, 5, 2, 3, 1).reshape(B, B_HEADS, B_HEAD_DIM, B_HEAD_DIM)


def _layer_norm(y, g, b):
    mu = jnp.mean(y, axis=-1, keepdims=True)
    var = jnp.mean(jnp.square(y - mu), axis=-1, keepdims=True)
    return (y - mu) * lax.rsqrt(var + LN_EPS) * g + b


def _outproj_kernel(oa_ref, o_ref, bonus_ref, g_ref, x_ref, wa_ref, wb_ref, gng_ref, gnb_ref,
                    lng_ref, lnb_ref, bd_ref, y_ref):
    bd = bd_ref[...]
    o = o_ref[...]
    inv = 1.0 / B_HEAD_DIM
    om = _head_sum(o, bd) * inv
    oc = o - om
    ov = _head_sum(oc * oc, bd) * inv
    ob = (oc * lax.rsqrt(ov + GN_EPS) * gng_ref[...] + gnb_ref[...] + bonus_ref[...]) * g_ref[...]
    mix = (jnp.dot(oa_ref[...].astype(BF16), wa_ref[...], preferred_element_type=F32)
           + jnp.dot(ob.astype(BF16), wb_ref[...], preferred_element_type=F32))
    y_ref[...] = _layer_norm(ALPHA * x_ref[...] + mix, lng_ref[...], lnb_ref[...])


def _outproj(oA, o, bonus, g, x, prm, tm):
    N = x.shape[0]
    tokA = pl.BlockSpec((tm, A_WIDTH), lambda i: (i, 0))
    tokD = pl.BlockSpec((tm, D_MODEL), lambda i: (i, 0))
    full = lambda a: pl.BlockSpec(a.shape, lambda i: (0, 0))
    return pl.pallas_call(
        _outproj_kernel,
        grid=(N // tm,),
        in_specs=[tokA, tokA, tokA, tokA, tokD, full(prm['wo_a']), full(prm['wo_b']),
                  full(prm['gn_g']), full(prm['gn_b']), full(prm['ln1_g']), full(prm['ln1_b']),
                  full(prm['bd'])],
        out_specs=tokD,
        out_shape=jax.ShapeDtypeStruct((N, D_MODEL), F32),
        compiler_params=_cparams(("parallel",)),
        name="out_proj_ln",
    )(oA, o, bonus, g, x, prm['wo_a'], prm['wo_b'], prm['gn_g'], prm['gn_b'],
      prm['ln1_g'], prm['ln1_b'], prm['bd'])


def _router_kernel(x_ref, rwh_ref, rwl_ref, rb_ref, gates_ref):
    xh, xl = _split(x_ref[...])
    nt = lambda a, b: lax.dot_general(a, b, (((1,), (1,)), ((), ())), preferred_element_type=F32)
    lg = nt(rwh_ref[...], xh) + nt(rwh_ref[...], xl) + nt(rwl_ref[...], xh)
    sc = jax.nn.sigmoid(lg)
    bi = sc + rb_ref[...]
    G, M = N_GROUPS, EXPERTS_PER_GROUP
    X = [bi[m * G:(m + 1) * G] for m in range(M)]
    S = [sc[m * G:(m + 1) * G] for m in range(M)]
    gs = None
    for a in range(M):
        for b in range(a + 1, M):
            pr = X[a] + X[b]
            gs = pr if gs is None else jnp.maximum(gs, pr)
    best = gs[0:1]
    gsel = jnp.zeros_like(best, dtype=jnp.int32)
    for g in range(1, G):
        c = gs[g:g + 1] > best
        gsel = jnp.where(c, g, gsel)
        best = jnp.where(c, gs[g:g + 1], best)
    vals, raws = [], []
    for m in range(M):
        vm, sm = X[m][0:1], S[m][0:1]
        for g in range(1, G):
            vm = jnp.where(gsel == g, X[m][g:g + 1], vm)
            sm = jnp.where(gsel == g, S[m][g:g + 1], sm)
        vals.append(vm)
        raws.append(sm)

    def first_argmax(vs):
        bv, bi_ = vs[0], jnp.zeros_like(gsel)
        for m in range(1, M):
            c = vs[m] > bv
            bi_ = jnp.where(c, m, bi_)
            bv = jnp.where(c, vs[m], bv)
        return bi_

    i1 = first_argmax(vals)
    i2 = first_argmax([jnp.where(i1 == m, -jnp.inf, vals[m]) for m in range(M)])
    w1 = sum(jnp.where(i1 == m, raws[m], 0.0) for m in range(M))
    w2 = sum(jnp.where(i2 == m, raws[m], 0.0) for m in range(M))
    den = w1 + w2
    g1, g2 = w1 / den, w2 / den
    for g in range(G):
        for m in range(M):
            val = jnp.where((gsel == g) & (i1 == m), g1, 0.0) + jnp.where((gsel == g) & (i2 == m), g2, 0.0)
            gates_ref[g * M + m:g * M + m + 1, :] = val


def _router(x, rw, rb, tm):
    N = x.shape[0]
    rwh, rwl = rw
    return pl.pallas_call(
        _router_kernel,
        grid=(N // tm,),
        in_specs=[pl.BlockSpec((tm, D_MODEL), lambda i: (i, 0)),
                  pl.BlockSpec(rwh.shape, lambda i: (0, 0)),
                  pl.BlockSpec(rwl.shape, lambda i: (0, 0)),
                  pl.BlockSpec(rb.shape, lambda i: (0, 0))],
        out_specs=pl.BlockSpec((N_EXPERTS, tm), lambda i: (0, i)),
        out_shape=jax.ShapeDtypeStruct((N_EXPERTS, N), F32),
        compiler_params=_cparams(("parallel",)),
        name="router",
    )(x, rwh, rwl, rb)


def _moe_kernel(x_ref, gate_ref, w1_ref, w3_ref, w2_ref, lng_ref, lnb_ref, y_ref, xb_scr, acc_scr):
    e = pl.program_id(1)
    f = pl.program_id(2)

    @pl.when((e == 0) & (f == 0))
    def _():
        xb_scr[...] = x_ref[...].astype(BF16)
        acc_scr[...] = jnp.zeros(acc_scr.shape, F32)

    xb = xb_scr[...]
    h1 = jnp.dot(xb, w1_ref[0], preferred_element_type=F32)
    h3 = jnp.dot(xb, w3_ref[0], preferred_element_type=F32)
    hh = (h1 * jax.nn.sigmoid(h1) * h3).astype(BF16)
    acc_scr[...] += gate_ref[0] * jnp.dot(hh, w2_ref[0], preferred_element_type=F32)

    @pl.when((e == pl.num_programs(1) - 1) & (f == pl.num_programs(2) - 1))
    def _():
        y_ref[...] = _layer_norm(ALPHA * x_ref[...] + acc_scr[...], lng_ref[...], lnb_ref[...])


def _moe(x, gates3, w1, w3, w2, lng, lnb, tm, tf):
    N = x.shape[0]
    full = lambda a: pl.BlockSpec(a.shape, lambda i, e, f: (0, 0))
    return pl.pallas_call(
        _moe_kernel,
        grid=(N // tm, N_EXPERTS, D_EXPERT // tf),
        in_specs=[pl.BlockSpec((tm, D_MODEL), lambda i, e, f: (i, 0)),
                  pl.BlockSpec((1, tm, 1), lambda i, e, f: (e, i, 0)),
                  pl.BlockSpec((1, D_MODEL, tf), lambda i, e, f: (e, 0, f)),
                  pl.BlockSpec((1, D_MODEL, tf), lambda i, e, f: (e, 0, f)),
                  pl.BlockSpec((1, tf, D_MODEL), lambda i, e, f: (e, f, 0)),
                  full(lng), full(lnb)],
        out_specs=pl.BlockSpec((tm, D_MODEL), lambda i, e, f: (i, 0)),
        out_shape=jax.ShapeDtypeStruct((N, D_MODEL), F32),
        scratch_shapes=[pltpu.VMEM((tm, D_MODEL), BF16), pltpu.VMEM((tm, D_MODEL), F32)],
        compiler_params=_cparams(("parallel", "arbitrary", "arbitrary")),
        name="moe_ln",
    )(x, gates3, w1, w3, w2, lng, lnb)


def _row(v, n=None):
    v = v.reshape(1, -1).astype(F32)
    if n is not None and v.shape[1] < n:
        v = jnp.pad(v, ((0, 0), (0, n - v.shape[1])))
    return v


def _layer_params(l, w_in, w_out, ln1_g, ln1_b, ln2_g, ln2_b, rwkv_mu, rwkv_w0, rwkv_w2, rwkv_a0,
                  rwkv_a2, rwkv_g2, rwkv_kk, rwkv_ka, rwkv_rk, rwkv_gn_g, rwkv_gn_b,
                  exp_w1, exp_w3, exp_w2):
    wi = w_in[l]
    offs = np.cumsum((0, A_WIDTH, A_KV_WIDTH, A_KV_WIDTH, IDX_WIDTH, IDX_DIM, IDX_HEADS))
    q, k, v, qi, ki, wgt = (wi[:, offs[n]:offs[n + 1]] for n in range(6))
    w_a1 = jnp.concatenate([q, k, v], axis=1)
    w_a2 = jnp.concatenate([qi, ki, wgt, jnp.zeros((D_MODEL, LANES - IDX_DIM - IDX_HEADS), F32)], axis=1)
    w_a2h, w_a2l = _split(w_a2)
    w_b = jnp.pad(wi[:, A_PROJ:], ((0, 0), (0, B_PAD - B_PROJ)))
    zl = jnp.zeros((DECAY_LORA, B_WIDTH), F32)
    seg = np.arange(LANES) // B_HEAD_DIM
    return dict(
        w_a1=w_a1.astype(BF16), w_a2h=w_a2h, w_a2l=w_a2l, w_b=w_b.astype(BF16),
        wo_a=w_out[l][:A_WIDTH].astype(BF16), wo_b=w_out[l][A_WIDTH:].astype(BF16),
        ln1_g=_row(ln1_g[l]), ln1_b=_row(ln1_b[l]), ln2_g=_row(ln2_g[l]), ln2_b=_row(ln2_b[l]),
        mu=_row(rwkv_mu[l], B_PAD), w0=_row(rwkv_w0[l]), a0=_row(rwkv_a0[l]),
        ww=jnp.concatenate([rwkv_w2[l], zl], axis=0).astype(BF16),
        wa=jnp.concatenate([zl, rwkv_a2[l]], axis=0).astype(BF16),
        wg=jnp.pad(rwkv_g2[l], ((0, 2 * LANES - GATE_LORA), (0, 0))).astype(BF16),
        kk=_row(rwkv_kk[l]), ka=_row(rwkv_ka[l]), rk=_row(rwkv_rk[l]),
        gn_g=_row(rwkv_gn_g[l]), gn_b=_row(rwkv_gn_b[l]),
        bd=jnp.asarray(seg[:, None] == seg[None, :], BF16),
        w1=exp_w1[l].astype(BF16), w3=exp_w3[l].astype(BF16), w2=exp_w2[l].astype(BF16),
    )


def _trunk_layer(x, prm, rw, rb, past):
    B, T, _ = x.shape
    N = B * T
    xt = x.reshape(N, D_MODEL)
    tm = min(512, N)
    hA1 = _matmul(xt, prm['w_a1'], tm, A1_PAD // 2).reshape(B, T, A1_PAD)
    hA2 = _matmul(xt, prm['w_a2h'], tm, A2_PAD // 3, prm['w_a2l']).reshape(B, T, A2_PAD)
    hB = _matmul(xt, prm['w_b'], tm, B_PAD // 3).reshape(B, T, B_PAD)
    k_new = hA1[..., A1_OFF_K:A1_OFF_K + A_KV_WIDTH]
    v_new = hA1[..., A1_OFF_V:A1_OFF_V + A_KV_WIDTH]
    ki_new = hA2[..., A2_OFF_KW:A2_OFF_KW + IDX_DIM]

    if past is None:
        kf, vf, kif = k_new, v_new, ki_new
        pos0, TQ = 0, min(Q_BLOCK, T)
        S0 = jnp.zeros((B, B_HEADS, B_HEAD_DIM, B_HEAD_DIM), F32)
        prev = jnp.zeros((B, 1, B_PAD), F32)
    else:
        ck, cv, cki, S0, prev = past
        P = ck.shape[1]
        kf = jnp.concatenate([ck.reshape(B, P, A_KV_WIDTH), k_new], axis=1)
        vf = jnp.concatenate([cv.reshape(B, P, A_KV_WIDTH), v_new], axis=1)
        kif = jnp.concatenate([cki, ki_new], axis=1)
        pos0, TQ = P, T
        prev = jnp.pad(prev, ((0, 0), (0, 0), (0, B_PAD - B_PROJ)))
    L = kf.shape[1]
    ksel = min(TOPK_MAX, L // 4)
    TK = min(512, -(-L // LANES) * LANES)
    Lp = -(-L // TK) * TK
    padk = lambda a: jnp.pad(a.astype(BF16), ((0, 0), (0, Lp - L), (0, 0)))
    ki_hi, ki_lo = _split(kif.astype(F32))
    ki3 = jnp.concatenate([ki_hi, ki_hi, ki_lo, jnp.zeros_like(ki_hi)], axis=-1)
    oA = _dsa(hA1, hA2, padk(kf), padk(vf), padk(ki3), TQ=TQ, L=L, pos0=pos0, ksel=ksel)

    r, d, k, v, q, b, g, bonus = _rwkv_prep(hB, prev, prm, min(256, T))
    o_s, hT = _rwkv_scan(_to_scan_j(q, B, T), _to_scan_j(d, B, T), _to_scan_j(b, B, T),
                         _to_scan_j(k, B, T), _to_scan_j(r, B, T), _to_scan_i(v, B, T),
                         _state_to_scan(S0.astype(F32), B), min(32, T))
    o = _from_scan_i(o_s, B, T)
    S_T = _state_from_scan(hT, B)
    last = hB[:, -1:, :B_PROJ]

    tmo = min(256, N)
    x1 = _outproj(oA.reshape(N, A_WIDTH), o.reshape(N, B_WIDTH), bonus.reshape(N, B_WIDTH),
                  g.reshape(N, B_WIDTH), xt, prm, tmo)
    gates = _router(x1, rw, rb, tm)
    x2 = _moe(x1, gates[:, :, None], prm['w1'], prm['w3'], prm['w2'], prm['ln2_g'], prm['ln2_b'],
              tm, 512)
    state = (k_new.reshape(B, T, A_KV_HEADS, A_HEAD_DIM), v_new.reshape(B, T, A_KV_HEADS, A_HEAD_DIM),
             ki_new, S_T, last)
    return x2.reshape(B, T, D_MODEL), state


def kernel(x_prompt, x_sample, cache_k, cache_v, cache_kidx, state_wkv, state_shift, w_in, w_out, ln1_g, ln1_b, ln2_g, ln2_b, rwkv_mu, rwkv_w0, rwkv_w2, rwkv_a0, rwkv_a2, rwkv_g2, rwkv_kk, rwkv_ka, rwkv_rk, rwkv_gn_g, rwkv_gn_b, router_w, router_b, exp_w1, exp_w3, exp_w2):
    assert x_prompt.shape[0] % 4 == 0 and x_sample.shape[0] % 4 == 0
    perm = np.array([g * EXPERTS_PER_GROUP + m for m in range(EXPERTS_PER_GROUP) for g in range(N_GROUPS)])
    rw = _split(router_w.T[perm].astype(F32))
    rb = router_b[perm].reshape(N_EXPERTS, 1).astype(F32)
    yp, ys = x_prompt, x_sample
    new_p, new_s = [], []
    for l in range(DEPTH):
        prm = _layer_params(l, w_in, w_out, ln1_g, ln1_b, ln2_g, ln2_b, rwkv_mu, rwkv_w0, rwkv_w2,
                            rwkv_a0, rwkv_a2, rwkv_g2, rwkv_kk, rwkv_ka, rwkv_rk, rwkv_gn_g, rwkv_gn_b,
                            exp_w1, exp_w3, exp_w2)
        yp, st_p = _trunk_layer(yp, prm, rw, rb, None)
        ys, st_s = _trunk_layer(ys, prm, rw, rb, (cache_k[l], cache_v[l], cache_kidx[l],
                                                   state_wkv[l], state_shift[l]))
        new_p.append(st_p)
        new_s.append(st_s)
    stack = lambda sts, n: jnp.stack([s[n] for s in sts])
    return (yp, ys) + tuple(stack(new_p, n) for n in range(5)) + tuple(stack(new_s, n) for n in range(5))
```

```python
import functools

import numpy as np
import jax
import jax.numpy as jnp
from jax import lax
from jax.experimental import pallas as pl
from jax.experimental.pallas import tpu as pltpu

F32 = jnp.float32
BF16 = jnp.bfloat16

D_MODEL = 2048
CHUNK = 64
Q_BLOCK = 128
A_HEADS = 8
A_KV_HEADS = 2
A_HEAD_DIM = 128
A_WIDTH = A_HEADS * A_HEAD_DIM
A_KV_WIDTH = A_KV_HEADS * A_HEAD_DIM
IDX_HEADS = 16
IDX_DIM = 64
IDX_WIDTH = IDX_HEADS * IDX_DIM
TOPK_MAX = 256
B_HEADS = 16
B_HEAD_DIM = 64
B_WIDTH = B_HEADS * B_HEAD_DIM
DECAY_LORA = 64
ICL_LORA = 64
GATE_LORA = 160
GN_EPS = 64e-5
N_EXPERTS = 16
N_GROUPS = 4
EXPERTS_PER_GROUP = 4
D_EXPERT = 1024
LN_EPS = 1e-5
DEPTH = 2
ALPHA = (2 * DEPTH) ** 0.25
A_PROJ = A_WIDTH + 2 * A_KV_WIDTH + IDX_WIDTH + IDX_DIM + IDX_HEADS
B_PROJ = 3 * B_WIDTH + DECAY_LORA + ICL_LORA + GATE_LORA

LANES = 128
SUBLANES = 8
VMEM_LIMIT = 56 * 1024 * 1024

A1_PAD = A_WIDTH + 2 * A_KV_WIDTH
A1_OFF_K, A1_OFF_V = A_WIDTH, A_WIDTH + A_KV_WIDTH
A2_PAD = IDX_WIDTH + LANES
A2_OFF_KW = IDX_WIDTH
B_PAD = 3 * B_WIDTH + LANES + 2 * LANES
B_OFF_LORA = 3 * B_WIDTH
B_OFF_GATE = B_OFF_LORA + LANES

INT_MIN = np.int32(-2 ** 31)
INT_MAX = np.int32(2 ** 31 - 1)
MASKED_DIST = -1e33
INIT_MAX = -1e30


def _cparams(sem):
    return pltpu.CompilerParams(dimension_semantics=sem, vmem_limit_bytes=VMEM_LIMIT)


def _mm_kernel(x_ref, w_ref, o_ref):
    o_ref[...] = jnp.dot(x_ref[...].astype(BF16), w_ref[...], preferred_element_type=F32)


def _split(x):
    hi = x.astype(BF16)
    return hi, (x - hi.astype(F32)).astype(BF16)


def _mm_split_kernel(x_ref, wh_ref, wl_ref, o_ref):
    hi, lo = _split(x_ref[...])
    wh = wh_ref[...]
    o_ref[...] = (jnp.dot(hi, wh, preferred_element_type=F32) + jnp.dot(lo, wh, preferred_element_type=F32)
                  + jnp.dot(hi, wl_ref[...], preferred_element_type=F32))


def _matmul(x, w, tm, tn, w_lo=None):
    M, K = x.shape
    N = w.shape[1]
    wspec = pl.BlockSpec((K, tn), lambda j, i: (0, j))
    ws = (w,) if w_lo is None else (w, w_lo)
    return pl.pallas_call(
        _mm_kernel if w_lo is None else _mm_split_kernel,
        grid=(N // tn, M // tm),
        in_specs=[pl.BlockSpec((tm, K), lambda j, i: (i, 0))] + [wspec] * len(ws),
        out_specs=pl.BlockSpec((tm, tn), lambda j, i: (i, j)),
        out_shape=jax.ShapeDtypeStruct((M, N), F32),
        compiler_params=_cparams(("parallel", "parallel")),
        name="in_proj" if w_lo is None else "in_proj_split",
    )(x, *ws)


def _dsa_kernel(q_ref, qi_ref, kw_ref, k_ref, v_ref, ki_ref, o_ref,
                key_scr, qim_scr, wb_scr, qs_scr, p_scr, m_scr, l_scr, acc_scr,
                *, TQ, TK, KC, L, pos0, ksel, idx_bits):
    i = pl.program_id(1)
    qpos0 = pos0 + i * TQ
    row = lax.broadcasted_iota(jnp.int32, (TQ, 1), 0)
    qpos = qpos0 + row
    limit = jnp.minimum((qpos // CHUNK + 1) * CHUNK, L)
    limit_max = jnp.minimum(((qpos0 + TQ - 1) // CHUNK + 1) * CHUNK, L)
    nkt = (limit_max + TK - 1) // TK
    lane = lax.broadcasted_iota(jnp.int32, (1, TK), 1)

    kw = kw_ref[0]
    half = lax.broadcasted_iota(jnp.int32, (1, LANES), 1) < IDX_DIM
    for p in range(IDX_HEADS // 2):
        slab = qi_ref[0, :, p * LANES:(p + 1) * LANES]
        hi = slab.astype(BF16).astype(F32)
        lo = slab - hi
        hi_sw = pltpu.roll(hi, IDX_DIM, 1)
        lo_sw = pltpu.roll(lo, IDX_DIM, 1)
        ra = slice(2 * p * TQ, (2 * p + 1) * TQ)
        rb = slice((2 * p + 1) * TQ, (2 * p + 2) * TQ)
        qim_scr[ra, 0:LANES] = jnp.where(half, hi, lo_sw).astype(BF16)
        qim_scr[ra, LANES:2 * LANES] = jnp.where(half, hi, 0.0).astype(BF16)
        qim_scr[rb, 0:LANES] = jnp.where(half, hi_sw, lo).astype(BF16)
        qim_scr[rb, LANES:2 * LANES] = jnp.where(half, hi_sw, 0.0).astype(BF16)
    for h in range(IDX_HEADS):
        wb_scr[h] = jnp.broadcast_to(kw[:, IDX_DIM + h:IDX_DIM + h + 1] * (IDX_WIDTH ** -0.5), (TQ, LANES))
    lane1 = lax.broadcasted_iota(jnp.int32, (1, LANES), 1)
    nt_dims = (((1,), (1,)), ((), ()))

    def score_tile(kt, carry):
        ks = pl.multiple_of(kt * TK, TK)
        for c in range(TK // KC):
            k0 = pl.multiple_of(ks + c * KC, KC)
            ki_t = ki_ref[0, pl.ds(k0, KC), :]
            accs = [jnp.zeros((TQ, LANES), F32) for _ in range(KC // LANES)]
            d_all = lax.dot_general(qim_scr[...], ki_t, nt_dims, preferred_element_type=F32)
            for h in range(IDX_HEADS):
                d = d_all[h * TQ:(h + 1) * TQ]
                wb = wb_scr[h]
                for s in range(KC // LANES):
                    accs[s] = accs[s] + jnp.maximum(d[:, s * LANES:(s + 1) * LANES], 0.0) * wb
            for s in range(KC // LANES):
                acc = accs[s]
                bits = pltpu.bitcast(acc, jnp.int32)
                key = bits ^ ((bits >> 31) & INT_MAX)
                key = jnp.where(acc == 0.0, 0, key)
                kpos = k0 + s * LANES + lane1
                key_scr[:, pl.ds(pl.multiple_of(k0 + s * LANES, LANES), LANES)] = (
                    jnp.where(kpos < limit, key, INT_MIN))
        return carry

    lax.fori_loop(0, nkt, score_tile, 0)

    def count(pred_fn):
        def body(kt, cnt):
            ks = pl.multiple_of(kt * TK, TK)
            m = pred_fn(key_scr[:, pl.ds(ks, TK)], ks)
            c = jnp.where(m, 1.0, 0.0)
            for s in range(TK // LANES):
                cnt = cnt + c[:, s * LANES:(s + 1) * LANES]
            return cnt
        cnt = lax.fori_loop(0, nkt, body, jnp.zeros((TQ, LANES), F32))
        return jnp.sum(cnt, axis=1, keepdims=True)

    kself = float(ksel)

    def bit_step(b, thr):
        cand = thr + jnp.left_shift(jnp.int32(1), 31 - b)
        c = count(lambda key, ks: key >= cand)
        return jnp.where(c >= kself, cand, thr)

    thr = lax.fori_loop(0, 32, bit_step, jnp.full((TQ, 1), INT_MIN, jnp.int32))
    n_gt = count(lambda key, ks: key > thr)
    n_ge = count(lambda key, ks: key >= thr)
    need = kself - n_gt

    def idx_step(b, jm):
        cand = jm + jnp.left_shift(jnp.int32(1), idx_bits - 1 - b)
        c = count(lambda key, ks: (key == thr) & ((ks + lane) < cand))
        return jnp.where(c < need, cand, jm)

    excess = n_ge > kself
    jm = lax.cond(jnp.max(jnp.where(excess, 1.0, 0.0)) > 0.0,
                  lambda: lax.fori_loop(0, idx_bits, idx_step, jnp.zeros((TQ, 1), jnp.int32)),
                  lambda: jnp.zeros((TQ, 1), jnp.int32))
    jm = jnp.where(excess, jm, INT_MAX)

    grp = A_HEADS // A_KV_HEADS
    log2e = float(np.log2(np.e))
    qscale = A_HEAD_DIM ** -0.5 * log2e
    for n in range(A_KV_HEADS):
        for g in range(grp):
            hh = n * grp + g
            qs_scr[n, g * TQ:(g + 1) * TQ, :] = (
                q_ref[0, :, hh * A_HEAD_DIM:(hh + 1) * A_HEAD_DIM] * qscale).astype(BF16)
    m_scr[...] = jnp.full(m_scr.shape, INIT_MAX, F32)
    l_scr[...] = jnp.zeros(l_scr.shape, F32)
    acc_scr[...] = jnp.zeros(acc_scr.shape, F32)
    slopes2 = [float(2.0 ** (-8.0 * (h + 1.0) / A_HEADS)) * log2e for h in range(A_HEADS)]
    lane_kc = lax.broadcasted_iota(jnp.int32, (1, KC), 1)

    def attn_tile(kt, carry):
        ks = pl.multiple_of(kt * TK, TK)
        for c in range(TK // KC):
            k0 = pl.multiple_of(ks + c * KC, KC)
            key = key_scr[:, pl.ds(k0, KC)]
            kpos = k0 + lane_kc
            sel = ((key > thr) | ((key == thr) & (kpos <= jm))) & (kpos < limit)
            ndm = jnp.where(sel, -jnp.abs(qpos - kpos).astype(F32), MASKED_DIST)
            for n in range(A_KV_HEADS):
                k_t = k_ref[0, pl.ds(k0, KC), n * A_HEAD_DIM:(n + 1) * A_HEAD_DIM]
                v_t = v_ref[0, pl.ds(k0, KC), n * A_HEAD_DIM:(n + 1) * A_HEAD_DIM]
                s_all = lax.dot_general(qs_scr[n], k_t, nt_dims, preferred_element_type=F32)
                for g in range(grp):
                    hh = n * grp + g
                    lg = s_all[g * TQ:(g + 1) * TQ] + slopes2[hh] * ndm
                    m_prev = m_scr[hh]
                    mx = lg[:, 0:LANES]
                    for s in range(1, KC // LANES):
                        mx = jnp.maximum(mx, lg[:, s * LANES:(s + 1) * LANES])
                    m_new = jnp.maximum(m_prev, jnp.max(mx, axis=1, keepdims=True))
                    alpha = jnp.exp2(m_prev - m_new)
                    ps = [jnp.exp2(lg[:, s * LANES:(s + 1) * LANES] - m_new) for s in range(KC // LANES)]
                    l_new = alpha * l_scr[hh]
                    for pp in ps:
                        l_new = l_new + pp
                    l_scr[hh] = l_new
                    p_scr[g * TQ:(g + 1) * TQ, :] = jnp.concatenate(ps, axis=1).astype(BF16)
                    acc_scr[hh] = alpha * acc_scr[hh]
                    m_scr[hh] = m_new
                pv = jnp.dot(p_scr[...], v_t, preferred_element_type=F32)
                for g in range(grp):
                    acc_scr[n * grp + g] += pv[g * TQ:(g + 1) * TQ]
        return carry

    lax.fori_loop(0, nkt, attn_tile, 0)
    for hh in range(A_HEADS):
        o_ref[0, :, hh * A_HEAD_DIM:(hh + 1) * A_HEAD_DIM] = (
            acc_scr[hh] / jnp.sum(l_scr[hh], axis=1, keepdims=True))


def _dsa(hA1, hA2, kb, vb, kib, *, TQ, L, pos0, ksel):
    B, T, _ = hA1.shape
    Lp = kb.shape[1]
    TK = min(512, Lp)
    assert Lp % TK == 0 and TK >= ksel and T % TQ == 0
    idx_bits = int(Lp).bit_length()
    KC = min(2 * LANES, TK)
    kern = functools.partial(_dsa_kernel, TQ=TQ, TK=TK, KC=KC, L=L, pos0=pos0, ksel=ksel, idx_bits=idx_bits)
    return pl.pallas_call(
        kern,
        grid=(B, T // TQ),
        in_specs=[
            pl.BlockSpec((1, TQ, A_WIDTH), lambda b, i: (b, i, 0)),
            pl.BlockSpec((1, TQ, IDX_WIDTH), lambda b, i: (b, i, 0)),
            pl.BlockSpec((1, TQ, LANES), lambda b, i: (b, i, A2_OFF_KW // LANES)),
            pl.BlockSpec((1, Lp, A_KV_WIDTH), lambda b, i: (b, 0, 0)),
            pl.BlockSpec((1, Lp, A_KV_WIDTH), lambda b, i: (b, 0, 0)),
            pl.BlockSpec((1, Lp, 2 * LANES), lambda b, i: (b, 0, 0)),
        ],
        out_specs=pl.BlockSpec((1, TQ, A_WIDTH), lambda b, i: (b, i, 0)),
        out_shape=jax.ShapeDtypeStruct((B, T, A_WIDTH), F32),
        scratch_shapes=[
            pltpu.VMEM((TQ, Lp), jnp.int32),
            pltpu.VMEM((IDX_HEADS * TQ, 2 * LANES), BF16),
            pltpu.VMEM((IDX_HEADS, TQ, LANES), F32),
            pltpu.VMEM((A_KV_HEADS, (A_HEADS // A_KV_HEADS) * TQ, A_HEAD_DIM), BF16),
            pltpu.VMEM(((A_HEADS // A_KV_HEADS) * TQ, KC), BF16),
            pltpu.VMEM((A_HEADS, TQ, LANES), F32),
            pltpu.VMEM((A_HEADS, TQ, LANES), F32),
            pltpu.VMEM((A_HEADS, TQ, A_HEAD_DIM), F32),
        ],
        compiler_params=_cparams(("parallel", "arbitrary")),
        name="dsa",
    )(hA1, hA2, hA2, kb, vb, kib)


def _head_sum(x, bd):
    hi = x.astype(BF16)
    lo = (x - hi.astype(F32)).astype(BF16)
    outs = []
    for s in range(x.shape[1] // LANES):
        sl = slice(s * LANES, (s + 1) * LANES)
        outs.append(jnp.dot(hi[:, sl], bd, preferred_element_type=F32)
                    + jnp.dot(lo[:, sl], bd, preferred_element_type=F32))
    return jnp.concatenate(outs, axis=1)


def _rwkv_prep_kernel(hb_ref, halo_ref, prev_ref, mu_ref, w0_ref, ww_ref, a0_ref, wa_ref, wg_ref,
                      kk_ref, ka_ref, rk_ref, bd_ref,
                      r_out, d_out, k_out, v_out, q_out, b_out, g_out, bonus_out):
    i = pl.program_id(1)
    p = hb_ref[0]
    tb = p.shape[0]
    first_prev = jnp.where(i == 0, prev_ref[0], halo_ref[0, SUBLANES - 1:SUBLANES, :])
    rolled = pltpu.roll(p, 1, 0)
    rowid = lax.broadcasted_iota(jnp.int32, (tb, 1), 0)
    p_prev = jnp.where(rowid == 0, first_prev, rolled)
    ps = p + (p_prev - p) * mu_ref[...]
    r = ps[:, 0:B_WIDTH]
    k = ps[:, B_WIDTH:2 * B_WIDTH]
    v = ps[:, 2 * B_WIDTH:3 * B_WIDTH]
    lora = ps[:, B_OFF_LORA:B_OFF_LORA + LANES]
    gl = ps[:, B_OFF_GATE:B_OFF_GATE + 2 * LANES]
    wz = w0_ref[...] + jnp.dot(jnp.tanh(lora).astype(BF16), ww_ref[...], preferred_element_type=F32)
    w = -jax.nn.softplus(-wz) - 0.5
    decay = jnp.exp(-jnp.exp(w))
    a = jax.nn.sigmoid(a0_ref[...] + jnp.dot(lora.astype(BF16), wa_ref[...], preferred_element_type=F32))
    g = jnp.dot(jax.nn.sigmoid(gl).astype(BF16), wg_ref[...], preferred_element_type=F32)
    bd = bd_ref[...]
    kk = k * kk_ref[...]
    nrm = jnp.maximum(jnp.sqrt(_head_sum(kk * kk, bd)), 1e-12)
    kk = kk / nrm
    k_mod = k * (1.0 + (a - 1.0) * ka_ref[...])
    bonus = _head_sum(r * k_mod * rk_ref[...], bd) * v
    r_out[0] = r
    d_out[0] = decay
    k_out[0] = k_mod
    v_out[0] = v
    q_out[0] = -kk
    b_out[0] = kk * a
    g_out[0] = g
    bonus_out[0] = bonus


def _rwkv_prep(hB, prev, prm, tb):
    B, T, _ = hB.shape
    nh = tb // SUBLANES
    row = lambda n: pl.BlockSpec((1, n), lambda b, i: (0, 0))
    full = lambda a: pl.BlockSpec(a.shape, lambda b, i: (0, 0))
    tok = pl.BlockSpec((1, tb, B_WIDTH), lambda b, i: (b, i, 0))
    return pl.pallas_call(
        _rwkv_prep_kernel,
        grid=(B, T // tb),
        in_specs=[
            pl.BlockSpec((1, tb, B_PAD), lambda b, i: (b, i, 0)),
            pl.BlockSpec((1, SUBLANES, B_PAD), lambda b, i: (b, jnp.maximum(i * nh - 1, 0), 0)),
            pl.BlockSpec((1, 1, B_PAD), lambda b, i: (b, 0, 0)),
            row(B_PAD), row(B_WIDTH), full(prm['ww']), row(B_WIDTH), full(prm['wa']), full(prm['wg']),
            row(B_WIDTH), row(B_WIDTH), row(B_WIDTH), full(prm['bd']),
        ],
        out_specs=[tok] * 8,
        out_shape=[jax.ShapeDtypeStruct((B, T, B_WIDTH), F32)] * 8,
        compiler_params=_cparams(("parallel", "parallel")),
        name="rwkv_prep",
    )(hB, hB, prev, prm['mu'], prm['w0'], prm['ww'], prm['a0'], prm['wa'], prm['wg'],
      prm['kk'], prm['ka'], prm['rk'], prm['bd'])


def _rwkv_scan_kernel(q_ref, d_ref, b_ref, k_ref, r_ref, v_ref, h0_ref, o_ref, hT_ref, h_scr, *, Tc):
    c = pl.program_id(1)
    JH = B_HEAD_DIM // 2

    @pl.when(c == 0)
    def _():
        h_scr[...] = h0_ref[0]

    def token(t, carry):
        u = jnp.zeros((B_HEAD_DIM, LANES), F32)
        for j in range(JH):
            u = u + q_ref[0, t, j:j + 1, :] * h_scr[j]
        u = u + pltpu.roll(u, LANES // 2, 1)
        vt = v_ref[0, t]
        o = jnp.zeros((B_HEAD_DIM, LANES), F32)
        for j in range(JH):
            hn = (h_scr[j] * d_ref[0, t, j:j + 1, :] + u * b_ref[0, t, j:j + 1, :]
                  + vt * k_ref[0, t, j:j + 1, :])
            h_scr[j] = hn
            o = o + hn * r_ref[0, t, j:j + 1, :]
        o_ref[0, t] = o + pltpu.roll(o, LANES // 2, 1)
        return carry

    lax.fori_loop(0, Tc, token, 0)

    @pl.when(c == pl.num_programs(1) - 1)
    def _():
        hT_ref[0] = h_scr[...]


def _rwkv_scan(q, d, b, k, r, v, h0, Tc):
    G, T = q.shape[:2]
    JH = B_HEAD_DIM // 2
    jspec = pl.BlockSpec((1, Tc, JH, LANES), lambda g, c: (g, c, 0, 0))
    ispec = pl.BlockSpec((1, Tc, B_HEAD_DIM, LANES), lambda g, c: (g, c, 0, 0))
    hspec = pl.BlockSpec((1, JH, B_HEAD_DIM, LANES), lambda g, c: (g, 0, 0, 0))
    return pl.pallas_call(
        functools.partial(_rwkv_scan_kernel, Tc=Tc),
        grid=(G, T // Tc),
        in_specs=[jspec] * 5 + [ispec, hspec],
        out_specs=[ispec, hspec],
        out_shape=[jax.ShapeDtypeStruct((G, T, B_HEAD_DIM, LANES), F32),
                   jax.ShapeDtypeStruct((G, JH, B_HEAD_DIM, LANES), F32)],
        scratch_shapes=[pltpu.VMEM((JH, B_HEAD_DIM, LANES), F32)],
        compiler_params=_cparams(("parallel", "arbitrary")),
        name="rwkv_scan",
    )(q, d, b, k, r, v, h0)


def _to_scan_j(x, B, T):
    G = B // 4
    x = x.reshape(G, 4, T, B_HEADS, 2, B_HEAD_DIM // 2)
    return x.transpose(0, 2, 5, 4, 1, 3).reshape(G, T, B_HEAD_DIM // 2, LANES)


def _to_scan_i(x, B, T):
    G = B // 4
    x = x.reshape(G, 4, T, B_HEADS, B_HEAD_DIM).transpose(0, 2, 4, 1, 3).reshape(G, T, B_HEAD_DIM, 64)
    return jnp.concatenate([x, x], axis=-1)


def _from_scan_i(o, B, T):
    G = B // 4
    o = o[..., :64].reshape(G, T, B_HEAD_DIM, 4, B_HEADS).transpose(0, 3, 1, 4, 2)
    return o.reshape(B, T, B_WIDTH)


def _state_to_scan(S, B):
    G = B // 4
    S = S.reshape(G, 4, B_HEADS, B_HEAD_DIM, 2, B_HEAD_DIM // 2)
    return S.transpose(0, 5, 3, 4, 1, 2).reshape(G, B_HEAD_DIM // 2, B_HEAD_DIM, LANES)


def _state_from_scan(Hs, B):
    G = B // 4
    Hs = Hs.reshape(G, B_HEAD_DIM // 2, B_HEAD_DIM, 2, 4, B_HEADS)
    return Hs.transpose(0, 4, 5, 2, 3, 1).reshape(B, B_HEADS, B_HEAD_DIM, B_HEAD_DIM)


def _layer_norm(y, g, b):
    mu = jnp.mean(y, axis=-1, keepdims=True)
    var = jnp.mean(jnp.square(y - mu), axis=-1, keepdims=True)
    return (y - mu) * lax.rsqrt(var + LN_EPS) * g + b


def _outproj_kernel(oa_ref, o_ref, bonus_ref, g_ref, x_ref, wa_ref, wb_ref, gng_ref, gnb_ref,
                    lng_ref, lnb_ref, bd_ref, y_ref):
    bd = bd_ref[...]
    o = o_ref[...]
    inv = 1.0 / B_HEAD_DIM
    om = _head_sum(o, bd) * inv
    oc = o - om
    ov = _head_sum(oc * oc, bd) * inv
    ob = (oc * lax.rsqrt(ov + GN_EPS) * gng_ref[...] + gnb_ref[...] + bonus_ref[...]) * g_ref[...]
    mix = (jnp.dot(oa_ref[...].astype(BF16), wa_ref[...], preferred_element_type=F32)
           + jnp.dot(ob.astype(BF16), wb_ref[...], preferred_element_type=F32))
    y_ref[...] = _layer_norm(ALPHA * x_ref[...] + mix, lng_ref[...], lnb_ref[...])


def _outproj(oA, o, bonus, g, x, prm, tm):
    N = x.shape[0]
    tokA = pl.BlockSpec((tm, A_WIDTH), lambda i: (i, 0))
    tokD = pl.BlockSpec((tm, D_MODEL), lambda i: (i, 0))
    full = lambda a: pl.BlockSpec(a.shape, lambda i: (0, 0))
    return pl.pallas_call(
        _outproj_kernel,
        grid=(N // tm,),
        in_specs=[tokA, tokA, tokA, tokA, tokD, full(prm['wo_a']), full(prm['wo_b']),
                  full(prm['gn_g']), full(prm['gn_b']), full(prm['ln1_g']), full(prm['ln1_b']),
                  full(prm['bd'])],
        out_specs=tokD,
        out_shape=jax.ShapeDtypeStruct((N, D_MODEL), F32),
        compiler_params=_cparams(("parallel",)),
        name="out_proj_ln",
    )(oA, o, bonus, g, x, prm['wo_a'], prm['wo_b'], prm['gn_g'], prm['gn_b'],
      prm['ln1_g'], prm['ln1_b'], prm['bd'])


def _router_kernel(x_ref, rwh_ref, rwl_ref, rb_ref, gates_ref):
    xh, xl = _split(x_ref[...])
    nt = lambda a, b: lax.dot_general(a, b, (((1,), (1,)), ((), ())), preferred_element_type=F32)
    lg = nt(rwh_ref[...], xh) + nt(rwh_ref[...], xl) + nt(rwl_ref[...], xh)
    sc = jax.nn.sigmoid(lg)
    bi = sc + rb_ref[...]
    G, M = N_GROUPS, EXPERTS_PER_GROUP
    X = [bi[m * G:(m + 1) * G] for m in range(M)]
    S = [sc[m * G:(m + 1) * G] for m in range(M)]
    gs = None
    for a in range(M):
        for b in range(a + 1, M):
            pr = X[a] + X[b]
            gs = pr if gs is None else jnp.maximum(gs, pr)
    best = gs[0:1]
    gsel = jnp.zeros_like(best, dtype=jnp.int32)
    for g in range(1, G):
        c = gs[g:g + 1] > best
        gsel = jnp.where(c, g, gsel)
        best = jnp.where(c, gs[g:g + 1], best)
    vals, raws = [], []
    for m in range(M):
        vm, sm = X[m][0:1], S[m][0:1]
        for g in range(1, G):
            vm = jnp.where(gsel == g, X[m][g:g + 1], vm)
            sm = jnp.where(gsel == g, S[m][g:g + 1], sm)
        vals.append(vm)
        raws.append(sm)

    def first_argmax(vs):
        bv, bi_ = vs[0], jnp.zeros_like(gsel)
        for m in range(1, M):
            c = vs[m] > bv
            bi_ = jnp.where(c, m, bi_)
            bv = jnp.where(c, vs[m], bv)
        return bi_

    i1 = first_argmax(vals)
    i2 = first_argmax([jnp.where(i1 == m, -jnp.inf, vals[m]) for m in range(M)])
    w1 = sum(jnp.where(i1 == m, raws[m], 0.0) for m in range(M))
    w2 = sum(jnp.where(i2 == m, raws[m], 0.0) for m in range(M))
    den = w1 + w2
    g1, g2 = w1 / den, w2 / den
    for g in range(G):
        for m in range(M):
            val = jnp.where((gsel == g) & (i1 == m), g1, 0.0) + jnp.where((gsel == g) & (i2 == m), g2, 0.0)
            gates_ref[g * M + m:g * M + m + 1, :] = val


def _router(x, rw, rb, tm):
    N = x.shape[0]
    rwh, rwl = rw
    return pl.pallas_call(
        _router_kernel,
        grid=(N // tm,),
        in_specs=[pl.BlockSpec((tm, D_MODEL), lambda i: (i, 0)),
                  pl.BlockSpec(rwh.shape, lambda i: (0, 0)),
                  pl.BlockSpec(rwl.shape, lambda i: (0, 0)),
                  pl.BlockSpec(rb.shape, lambda i: (0, 0))],
        out_specs=pl.BlockSpec((N_EXPERTS, tm), lambda i: (0, i)),
        out_shape=jax.ShapeDtypeStruct((N_EXPERTS, N), F32),
        compiler_params=_cparams(("parallel",)),
        name="router",
    )(x, rwh, rwl, rb)


def _moe_kernel(x_ref, gate_ref, w1_ref, w3_ref, w2_ref, lng_ref, lnb_ref, y_ref, xb_scr, acc_scr):
    e = pl.program_id(1)
    f = pl.program_id(2)

    @pl.when((e == 0) & (f == 0))
    def _():
        xb_scr[...] = x_ref[...].astype(BF16)
        acc_scr[...] = jnp.zeros(acc_scr.shape, F32)

    xb = xb_scr[...]
    h1 = jnp.dot(xb, w1_ref[0], preferred_element_type=F32)
    h3 = jnp.dot(xb, w3_ref[0], preferred_element_type=F32)
    hh = (h1 * jax.nn.sigmoid(h1) * h3).astype(BF16)
    acc_scr[...] += gate_ref[0] * jnp.dot(hh, w2_ref[0], preferred_element_type=F32)

    @pl.when((e == pl.num_programs(1) - 1) & (f == pl.num_programs(2) - 1))
    def _():
        y_ref[...] = _layer_norm(ALPHA * x_ref[...] + acc_scr[...], lng_ref[...], lnb_ref[...])


def _moe(x, gates3, w1, w3, w2, lng, lnb, tm, tf):
    N = x.shape[0]
    full = lambda a: pl.BlockSpec(a.shape, lambda i, e, f: (0, 0))
    return pl.pallas_call(
        _moe_kernel,
        grid=(N // tm, N_EXPERTS, D_EXPERT // tf),
        in_specs=[pl.BlockSpec((tm, D_MODEL), lambda i, e, f: (i, 0)),
                  pl.BlockSpec((1, tm, 1), lambda i, e, f: (e, i, 0)),
                  pl.BlockSpec((1, D_MODEL, tf), lambda i, e, f: (e, 0, f)),
                  pl.BlockSpec((1, D_MODEL, tf), lambda i, e, f: (e, 0, f)),
                  pl.BlockSpec((1, tf, D_MODEL), lambda i, e, f: (e, f, 0)),
                  full(lng), full(lnb)],
        out_specs=pl.BlockSpec((tm, D_MODEL), lambda i, e, f: (i, 0)),
        out_shape=jax.ShapeDtypeStruct((N, D_MODEL), F32),
        scratch_shapes=[pltpu.VMEM((tm, D_MODEL), BF16), pltpu.VMEM((tm, D_MODEL), F32)],
        compiler_params=_cparams(("parallel", "arbitrary", "arbitrary")),
        name="moe_ln",
    )(x, gates3, w1, w3, w2, lng, lnb)


def _row(v, n=None):
    v = v.reshape(1, -1).astype(F32)
    if n is not None and v.shape[1] < n:
        v = jnp.pad(v, ((0, 0), (0, n - v.shape[1])))
    return v


def _layer_params(l, w_in, w_out, ln1_g, ln1_b, ln2_g, ln2_b, rwkv_mu, rwkv_w0, rwkv_w2, rwkv_a0,
                  rwkv_a2, rwkv_g2, rwkv_kk, rwkv_ka, rwkv_rk, rwkv_gn_g, rwkv_gn_b,
                  exp_w1, exp_w3, exp_w2):
    wi = w_in[l]
    offs = np.cumsum((0, A_WIDTH, A_KV_WIDTH, A_KV_WIDTH, IDX_WIDTH, IDX_DIM, IDX_HEADS))
    q, k, v, qi, ki, wgt = (wi[:, offs[n]:offs[n + 1]] for n in range(6))
    w_a1 = jnp.concatenate([q, k, v], axis=1)
    w_a2 = jnp.concatenate([qi, ki, wgt, jnp.zeros((D_MODEL, LANES - IDX_DIM - IDX_HEADS), F32)], axis=1)
    w_a2h, w_a2l = _split(w_a2)
    w_b = jnp.pad(wi[:, A_PROJ:], ((0, 0), (0, B_PAD - B_PROJ)))
    zl = jnp.zeros((DECAY_LORA, B_WIDTH), F32)
    seg = np.arange(LANES) // B_HEAD_DIM
    return dict(
        w_a1=w_a1.astype(BF16), w_a2h=w_a2h, w_a2l=w_a2l, w_b=w_b.astype(BF16),
        wo_a=w_out[l][:A_WIDTH].astype(BF16), wo_b=w_out[l][A_WIDTH:].astype(BF16),
        ln1_g=_row(ln1_g[l]), ln1_b=_row(ln1_b[l]), ln2_g=_row(ln2_g[l]), ln2_b=_row(ln2_b[l]),
        mu=_row(rwkv_mu[l], B_PAD), w0=_row(rwkv_w0[l]), a0=_row(rwkv_a0[l]),
        ww=jnp.concatenate([rwkv_w2[l], zl], axis=0).astype(BF16),
        wa=jnp.concatenate([zl, rwkv_a2[l]], axis=0).astype(BF16),
        wg=jnp.pad(rwkv_g2[l], ((0, 2 * LANES - GATE_LORA), (0, 0))).astype(BF16),
        kk=_row(rwkv_kk[l]), ka=_row(rwkv_ka[l]), rk=_row(rwkv_rk[l]),
        gn_g=_row(rwkv_gn_g[l]), gn_b=_row(rwkv_gn_b[l]),
        bd=jnp.asarray(seg[:, None] == seg[None, :], BF16),
        w1=exp_w1[l].astype(BF16), w3=exp_w3[l].astype(BF16), w2=exp_w2[l].astype(BF16),
    )


def _trunk_layer(x, prm, rw, rb, past):
    B, T, _ = x.shape
    N = B * T
    xt = x.reshape(N, D_MODEL)
    tm = min(512, N)
    hA1 = _matmul(xt, prm['w_a1'], tm, A1_PAD // 2).reshape(B, T, A1_PAD)
    hA2 = _matmul(xt, prm['w_a2h'], tm, A2_PAD // 3, prm['w_a2l']).reshape(B, T, A2_PAD)
    hB = _matmul(xt, prm['w_b'], tm, B_PAD // 3).reshape(B, T, B_PAD)
    k_new = hA1[..., A1_OFF_K:A1_OFF_K + A_KV_WIDTH]
    v_new = hA1[..., A1_OFF_V:A1_OFF_V + A_KV_WIDTH]
    ki_new = hA2[..., A2_OFF_KW:A2_OFF_KW + IDX_DIM]

    if past is None:
        kf, vf, kif = k_new, v_new, ki_new
        pos0, TQ = 0, min(Q_BLOCK, T)
        S0 = jnp.zeros((B, B_HEADS, B_HEAD_DIM, B_HEAD_DIM), F32)
        prev = jnp.zeros((B, 1, B_PAD), F32)
    else:
        ck, cv, cki, S0, prev = past
        P = ck.shape[1]
        kf = jnp.concatenate([ck.reshape(B, P, A_KV_WIDTH), k_new], axis=1)
        vf = jnp.concatenate([cv.reshape(B, P, A_KV_WIDTH), v_new], axis=1)
        kif = jnp.concatenate([cki, ki_new], axis=1)
        pos0, TQ = P, T
        prev = jnp.pad(prev, ((0, 0), (0, 0), (0, B_PAD - B_PROJ)))
    L = kf.shape[1]
    ksel = min(TOPK_MAX, L // 4)
    TK = min(512, -(-L // LANES) * LANES)
    Lp = -(-L // TK) * TK
    padk = lambda a: jnp.pad(a.astype(BF16), ((0, 0), (0, Lp - L), (0, 0)))
    ki_hi, ki_lo = _split(kif.astype(F32))
    ki3 = jnp.concatenate([ki_hi, ki_hi, ki_lo, jnp.zeros_like(ki_hi)], axis=-1)
    oA = _dsa(hA1, hA2, padk(kf), padk(vf), padk(ki3), TQ=TQ, L=L, pos0=pos0, ksel=ksel)

    r, d, k, v, q, b, g, bonus = _rwkv_prep(hB, prev, prm, min(256, T))
    o_s, hT = _rwkv_scan(_to_scan_j(q, B, T), _to_scan_j(d, B, T), _to_scan_j(b, B, T),
                         _to_scan_j(k, B, T), _to_scan_j(r, B, T), _to_scan_i(v, B, T),
                         _state_to_scan(S0.astype(F32), B), min(32, T))
    o = _from_scan_i(o_s, B, T)
    S_T = _state_from_scan(hT, B)
    last = hB[:, -1:, :B_PROJ]

    tmo = min(256, N)
    x1 = _outproj(oA.reshape(N, A_WIDTH), o.reshape(N, B_WIDTH), bonus.reshape(N, B_WIDTH),
                  g.reshape(N, B_WIDTH), xt, prm, tmo)
    gates = _router(x1, rw, rb, tm)
    x2 = _moe(x1, gates[:, :, None], prm['w1'], prm['w3'], prm['w2'], prm['ln2_g'], prm['ln2_b'],
              tm, 512)
    state = (k_new.reshape(B, T, A_KV_HEADS, A_HEAD_DIM), v_new.reshape(B, T, A_KV_HEADS, A_HEAD_DIM),
             ki_new, S_T, last)
    return x2.reshape(B, T, D_MODEL), state


def kernel(x_prompt, x_sample, cache_k, cache_v, cache_kidx, state_wkv, state_shift, w_in, w_out, ln1_g, ln1_b, ln2_g, ln2_b, rwkv_mu, rwkv_w0, rwkv_w2, rwkv_a0, rwkv_a2, rwkv_g2, rwkv_kk, rwkv_ka, rwkv_rk, rwkv_gn_g, rwkv_gn_b, router_w, router_b, exp_w1, exp_w3, exp_w2):
    assert x_prompt.shape[0] % 4 == 0 and x_sample.shape[0] % 4 == 0
    perm = np.array([g * EXPERTS_PER_GROUP + m for m in range(EXPERTS_PER_GROUP) for g in range(N_GROUPS)])
    rw = _split(router_w.T[perm].astype(F32))
    rb = router_b[perm].reshape(N_EXPERTS, 1).astype(F32)
    yp, ys = x_prompt, x_sample
    new_p, new_s = [], []
    for l in range(DEPTH):
        prm = _layer_params(l, w_in, w_out, ln1_g, ln1_b, ln2_g, ln2_b, rwkv_mu, rwkv_w0, rwkv_w2,
                            rwkv_a0, rwkv_a2, rwkv_g2, rwkv_kk, rwkv_ka, rwkv_rk, rwkv_gn_g, rwkv_gn_b,
                            exp_w1, exp_w3, exp_w2)
        yp, st_p = _trunk_layer(yp, prm, rw, rb, None)
        ys, st_s = _trunk_layer(ys, prm, rw, rb, (cache_k[l], cache_v[l], cache_kidx[l],
                                                   state_wkv[l], state_shift[l]))
        new_p.append(st_p)
        new_s.append(st_s)
    stack = lambda sts, n: jnp.stack([s[n] for s in sts])
    return (yp, ys) + tuple(stack(new_p, n) for n in range(5)) + tuple(stack(new_s, n) for n in range(5))
```

```python
import functools

import numpy as np
import jax
import jax.numpy as jnp
from jax import lax
from jax.experimental import pallas as pl
from jax.experimental.pallas import tpu as pltpu

F32 = jnp.float32
BF16 = jnp.bfloat16

D_MODEL = 2048
CHUNK = 64
Q_BLOCK = 128
A_HEADS = 8
A_KV_HEADS = 2
A_HEAD_DIM = 128
A_WIDTH = A_HEADS * A_HEAD_DIM
A_KV_WIDTH = A_KV_HEADS * A_HEAD_DIM
IDX_HEADS = 16
IDX_DIM = 64
IDX_WIDTH = IDX_HEADS * IDX_DIM
TOPK_MAX = 256
B_HEADS = 16
B_HEAD_DIM = 64
B_WIDTH = B_HEADS * B_HEAD_DIM
DECAY_LORA = 64
ICL_LORA = 64
GATE_LORA = 160
GN_EPS = 64e-5
N_EXPERTS = 16
N_GROUPS = 4
EXPERTS_PER_GROUP = 4
D_EXPERT = 1024
LN_EPS = 1e-5
DEPTH = 2
ALPHA = (2 * DEPTH) ** 0.25
A_PROJ = A_WIDTH + 2 * A_KV_WIDTH + IDX_WIDTH + IDX_DIM + IDX_HEADS
B_PROJ = 3 * B_WIDTH + DECAY_LORA + ICL_LORA + GATE_LORA

LANES = 128
SUBLANES = 8
VMEM_LIMIT = 56 * 1024 * 1024

A1_PAD = A_WIDTH + 2 * A_KV_WIDTH
A1_OFF_K, A1_OFF_V = A_WIDTH, A_WIDTH + A_KV_WIDTH
A2_PAD = IDX_WIDTH + LANES
A2_OFF_KW = IDX_WIDTH
B_PAD = 3 * B_WIDTH + LANES + 2 * LANES
B_OFF_LORA = 3 * B_WIDTH
B_OFF_GATE = B_OFF_LORA + LANES

INT_MIN = np.int32(-2 ** 31)
INT_MAX = np.int32(2 ** 31 - 1)
MASKED_DIST = -1e33
INIT_MAX = -1e30


def _cparams(sem):
    return pltpu.CompilerParams(dimension_semantics=sem, vmem_limit_bytes=VMEM_LIMIT)


def _mm_kernel(x_ref, w_ref, o_ref):
    o_ref[...] = jnp.dot(x_ref[...].astype(BF16), w_ref[...], preferred_element_type=F32)


def _split(x):
    hi = x.astype(BF16)
    return hi, (x - hi.astype(F32)).astype(BF16)


def _mm_split_kernel(x_ref, wh_ref, wl_ref, o_ref):
    hi, lo = _split(x_ref[...])
    wh = wh_ref[...]
    o_ref[...] = (jnp.dot(hi, wh, preferred_element_type=F32) + jnp.dot(lo, wh, preferred_element_type=F32)
                  + jnp.dot(hi, wl_ref[...], preferred_element_type=F32))


def _matmul(x, w, tm, tn, w_lo=None):
    M, K = x.shape
    N = w.shape[1]
    wspec = pl.BlockSpec((K, tn), lambda j, i: (0, j))
    ws = (w,) if w_lo is None else (w, w_lo)
    return pl.pallas_call(
        _mm_kernel if w_lo is None else _mm_split_kernel,
        grid=(N // tn, M // tm),
        in_specs=[pl.BlockSpec((tm, K), lambda j, i: (i, 0))] + [wspec] * len(ws),
        out_specs=pl.BlockSpec((tm, tn), lambda j, i: (i, j)),
        out_shape=jax.ShapeDtypeStruct((M, N), F32),
        compiler_params=_cparams(("parallel", "parallel")),
        name="in_proj" if w_lo is None else "in_proj_split",
    )(x, *ws)


def _dsa_kernel(q_ref, qi_ref, kw_ref, k_ref, v_ref, ki_ref, o_ref,
                key_scr, qim_scr, wb_scr, qs_scr, p_scr, m_scr, l_scr, acc_scr,
                *, TQ, TK, KC, L, pos0, ksel, idx_bits):
    i = pl.program_id(1)
    qpos0 = pos0 + i * TQ
    row = lax.broadcasted_iota(jnp.int32, (TQ, 1), 0)
    qpos = qpos0 + row
    limit = jnp.minimum((qpos // CHUNK + 1) * CHUNK, L)
    limit_max = jnp.minimum(((qpos0 + TQ - 1) // CHUNK + 1) * CHUNK, L)
    nkt = (limit_max + TK - 1) // TK
    lane = lax.broadcasted_iota(jnp.int32, (1, TK), 1)

    kw = kw_ref[0]
    half = lax.broadcasted_iota(jnp.int32, (1, LANES), 1) < IDX_DIM
    for p in range(IDX_HEADS // 2):
        slab = qi_ref[0, :, p * LANES:(p + 1) * LANES]
        hi = slab.astype(BF16).astype(F32)
        lo = slab - hi
        hi_sw = pltpu.roll(hi, IDX_DIM, 1)
        lo_sw = pltpu.roll(lo, IDX_DIM, 1)
        ra = slice(2 * p * TQ, (2 * p + 1) * TQ)
        rb = slice((2 * p + 1) * TQ, (2 * p + 2) * TQ)
        qim_scr[ra, 0:LANES] = jnp.where(half, hi, lo_sw).astype(BF16)
        qim_scr[ra, LANES:2 * LANES] = jnp.where(half, hi, 0.0).astype(BF16)
        qim_scr[rb, 0:LANES] = jnp.where(half, hi_sw, lo).astype(BF16)
        qim_scr[rb, LANES:2 * LANES] = jnp.where(half, hi_sw, 0.0).astype(BF16)
    for h in range(IDX_HEADS):
        wb_scr[h] = jnp.broadcast_to(kw[:, IDX_DIM + h:IDX_DIM + h + 1] * (IDX_WIDTH ** -0.5), (TQ, LANES))
    lane1 = lax.broadcasted_iota(jnp.int32, (1, LANES), 1)
    nt_dims = (((1,), (1,)), ((), ()))

    def score_tile(kt, carry):
        ks = pl.multiple_of(kt * TK, TK)
        for c in range(TK // KC):
            k0 = pl.multiple_of(ks + c * KC, KC)
            ki_t = ki_ref[0, pl.ds(k0, KC), :]
            accs = [jnp.zeros((TQ, LANES), F32) for _ in range(KC // LANES)]
            d_all = lax.dot_general(qim_scr[...], ki_t, nt_dims, preferred_element_type=F32)
            for h in range(IDX_HEADS):
                d = d_all[h * TQ:(h + 1) * TQ]
                wb = wb_scr[h]
                for s in range(KC // LANES):
                    accs[s] = accs[s] + jnp.maximum(d[:, s * LANES:(s + 1) * LANES], 0.0) * wb
            for s in range(KC // LANES):
                acc = accs[s]
                bits = pltpu.bitcast(acc, jnp.int32)
                key = bits ^ ((bits >> 31) & INT_MAX)
                key = jnp.where(acc == 0.0, 0, key)
                kpos = k0 + s * LANES + lane1
                key_scr[:, pl.ds(pl.multiple_of(k0 + s * LANES, LANES), LANES)] = (
                    jnp.where(kpos < limit, key, INT_MIN))
        return carry

    lax.fori_loop(0, nkt, score_tile, 0)

    def count(pred_fn):
        def body(kt, cnt):
            ks = pl.multiple_of(kt * TK, TK)
            m = pred_fn(key_scr[:, pl.ds(ks, TK)], ks)
            c = jnp.where(m, 1.0, 0.0)
            for s in range(TK // LANES):
                cnt = cnt + c[:, s * LANES:(s + 1) * LANES]
            return cnt
        cnt = lax.fori_loop(0, nkt, body, jnp.zeros((TQ, LANES), F32))
        return jnp.sum(cnt, axis=1, keepdims=True)

    kself = float(ksel)

    def bit_step(b, thr):
        cand = thr + jnp.left_shift(jnp.int32(1), 31 - b)
        c = count(lambda key, ks: key >= cand)
        return jnp.where(c >= kself, cand, thr)

    thr = lax.fori_loop(0, 32, bit_step, jnp.full((TQ, 1), INT_MIN, jnp.int32))
    n_gt = count(lambda key, ks: key > thr)
    n_ge = count(lambda key, ks: key >= thr)
    need = kself - n_gt

    def idx_step(b, jm):
        cand = jm + jnp.left_shift(jnp.int32(1), idx_bits - 1 - b)
        c = count(lambda key, ks: (key == thr) & ((ks + lane) < cand))
        return jnp.where(c < need, cand, jm)

    excess = n_ge > kself
    jm = lax.cond(jnp.max(jnp.where(excess, 1.0, 0.0)) > 0.0,
                  lambda: lax.fori_loop(0, idx_bits, idx_step, jnp.zeros((TQ, 1), jnp.int32)),
                  lambda: jnp.zeros((TQ, 1), jnp.int32))
    jm = jnp.where(excess, jm, INT_MAX)

    grp = A_HEADS // A_KV_HEADS
    log2e = float(np.log2(np.e))
    qscale = A_HEAD_DIM ** -0.5 * log2e
    for n in range(A_KV_HEADS):
        for g in range(grp):
            hh = n * grp + g
            qs_scr[n, g * TQ:(g + 1) * TQ, :] = (
                q_ref[0, :, hh * A_HEAD_DIM:(hh + 1) * A_HEAD_DIM] * qscale).astype(BF16)
    m_scr[...] = jnp.full(m_scr.shape, INIT_MAX, F32)
    l_scr[...] = jnp.zeros(l_scr.shape, F32)
    acc_scr[...] = jnp.zeros(acc_scr.shape, F32)
    slopes2 = [float(2.0 ** (-8.0 * (h + 1.0) / A_HEADS)) * log2e for h in range(A_HEADS)]
    lane_kc = lax.broadcasted_iota(jnp.int32, (1, KC), 1)

    def attn_tile(kt, carry):
        ks = pl.multiple_of(kt * TK, TK)
        for c in range(TK // KC):
            k0 = pl.multiple_of(ks + c * KC, KC)
            key = key_scr[:, pl.ds(k0, KC)]
            kpos = k0 + lane_kc
            sel = ((key > thr) | ((key == thr) & (kpos <= jm))) & (kpos < limit)
            ndm = jnp.where(sel, -jnp.abs(qpos - kpos).astype(F32), MASKED_DIST)
            for n in range(A_KV_HEADS):
                k_t = k_ref[0, pl.ds(k0, KC), n * A_HEAD_DIM:(n + 1) * A_HEAD_DIM]
                v_t = v_ref[0, pl.ds(k0, KC), n * A_HEAD_DIM:(n + 1) * A_HEAD_DIM]
                s_all = lax.dot_general(qs_scr[n], k_t, nt_dims, preferred_element_type=F32)
                for g in range(grp):
                    hh = n * grp + g
                    lg = s_all[g * TQ:(g + 1) * TQ] + slopes2[hh] * ndm
                    m_prev = m_scr[hh]
                    mx = lg[:, 0:LANES]
                    for s in range(1, KC // LANES):
                        mx = jnp.maximum(mx, lg[:, s * LANES:(s + 1) * LANES])
                    m_new = jnp.maximum(m_prev, jnp.max(mx, axis=1, keepdims=True))
                    alpha = jnp.exp2(m_prev - m_new)
                    ps = [jnp.exp2(lg[:, s * LANES:(s + 1) * LANES] - m_new) for s in range(KC // LANES)]
                    l_new = alpha * l_scr[hh]
                    for pp in ps:
                        l_new = l_new + pp
                    l_scr[hh] = l_new
                    p_scr[g * TQ:(g + 1) * TQ, :] = jnp.concatenate(ps, axis=1).astype(BF16)
                    acc_scr[hh] = alpha * acc_scr[hh]
                    m_scr[hh] = m_new
                pv = jnp.dot(p_scr[...], v_t, preferred_element_type=F32)
                for g in range(grp):
                    acc_scr[n * grp + g] += pv[g * TQ:(g + 1) * TQ]
        return carry

    lax.fori_loop(0, nkt, attn_tile, 0)
    for hh in range(A_HEADS):
        o_ref[0, :, hh * A_HEAD_DIM:(hh + 1) * A_HEAD_DIM] = (
            acc_scr[hh] / jnp.sum(l_scr[hh], axis=1, keepdims=True))


def _dsa(hA1, hA2, kb, vb, kib, *, TQ, L, pos0, ksel):
    B, T, _ = hA1.shape
    Lp = kb.shape[1]
    TK = min(512, Lp)
    assert Lp % TK == 0 and TK >= ksel and T % TQ == 0
    idx_bits = int(Lp).bit_length()
    KC = min(2 * LANES, TK)
    kern = functools.partial(_dsa_kernel, TQ=TQ, TK=TK, KC=KC, L=L, pos0=pos0, ksel=ksel, idx_bits=idx_bits)
    return pl.pallas_call(
        kern,
        grid=(B, T // TQ),
        in_specs=[
            pl.BlockSpec((1, TQ, A_WIDTH), lambda b, i: (b, i, 0)),
            pl.BlockSpec((1, TQ, IDX_WIDTH), lambda b, i: (b, i, 0)),
            pl.BlockSpec((1, TQ, LANES), lambda b, i: (b, i, A2_OFF_KW // LANES)),
            pl.BlockSpec((1, Lp, A_KV_WIDTH), lambda b, i: (b, 0, 0)),
            pl.BlockSpec((1, Lp, A_KV_WIDTH), lambda b, i: (b, 0, 0)),
            pl.BlockSpec((1, Lp, 2 * LANES), lambda b, i: (b, 0, 0)),
        ],
        out_specs=pl.BlockSpec((1, TQ, A_WIDTH), lambda b, i: (b, i, 0)),
        out_shape=jax.ShapeDtypeStruct((B, T, A_WIDTH), F32),
        scratch_shapes=[
            pltpu.VMEM((TQ, Lp), jnp.int32),
            pltpu.VMEM((IDX_HEADS * TQ, 2 * LANES), BF16),
            pltpu.VMEM((IDX_HEADS, TQ, LANES), F32),
            pltpu.VMEM((A_KV_HEADS, (A_HEADS // A_KV_HEADS) * TQ, A_HEAD_DIM), BF16),
            pltpu.VMEM(((A_HEADS // A_KV_HEADS) * TQ, KC), BF16),
            pltpu.VMEM((A_HEADS, TQ, LANES), F32),
            pltpu.VMEM((A_HEADS, TQ, LANES), F32),
            pltpu.VMEM((A_HEADS, TQ, A_HEAD_DIM), F32),
        ],
        compiler_params=_cparams(("parallel", "arbitrary")),
        name="dsa",
    )(hA1, hA2, hA2, kb, vb, kib)


def _head_sum(x, bd):
    hi = x.astype(BF16)
    lo = (x - hi.astype(F32)).astype(BF16)
    outs = []
    for s in range(x.shape[1] // LANES):
        sl = slice(s * LANES, (s + 1) * LANES)
        outs.append(jnp.dot(hi[:, sl], bd, preferred_element_type=F32)
                    + jnp.dot(lo[:, sl], bd, preferred_element_type=F32))
    return jnp.concatenate(outs, axis=1)


def _rwkv_prep_kernel(hb_ref, halo_ref, prev_ref, mu_ref, w0_ref, ww_ref, a0_ref, wa_ref, wg_ref,
                      kk_ref, ka_ref, rk_ref, bd_ref,
                      r_out, d_out, k_out, v_out, q_out, b_out, g_out, bonus_out):
    i = pl.program_id(1)
    p = hb_ref[0]
    tb = p.shape[0]
    first_prev = jnp.where(i == 0, prev_ref[0], halo_ref[0, SUBLANES - 1:SUBLANES, :])
    rolled = pltpu.roll(p, 1, 0)
    rowid = lax.broadcasted_iota(jnp.int32, (tb, 1), 0)
    p_prev = jnp.where(rowid == 0, first_prev, rolled)
    ps = p + (p_prev - p) * mu_ref[...]
    r = ps[:, 0:B_WIDTH]
    k = ps[:, B_WIDTH:2 * B_WIDTH]
    v = ps[:, 2 * B_WIDTH:3 * B_WIDTH]
    lora = ps[:, B_OFF_LORA:B_OFF_LORA + LANES]
    gl = ps[:, B_OFF_GATE:B_OFF_GATE + 2 * LANES]
    wz = w0_ref[...] + jnp.dot(jnp.tanh(lora).astype(BF16), ww_ref[...], preferred_element_type=F32)
    w = -jax.nn.softplus(-wz) - 0.5
    decay = jnp.exp(-jnp.exp(w))
    a = jax.nn.sigmoid(a0_ref[...] + jnp.dot(lora.astype(BF16), wa_ref[...], preferred_element_type=F32))
    g = jnp.dot(jax.nn.sigmoid(gl).astype(BF16), wg_ref[...], preferred_element_type=F32)
    bd = bd_ref[...]
    kk = k * kk_ref[...]
    nrm = jnp.maximum(jnp.sqrt(_head_sum(kk * kk, bd)), 1e-12)
    kk = kk / nrm
    k_mod = k * (1.0 + (a - 1.0) * ka_ref[...])
    bonus = _head_sum(r * k_mod * rk_ref[...], bd) * v
    r_out[0] = r
    d_out[0] = decay
    k_out[0] = k_mod
    v_out[0] = v
    q_out[0] = -kk
    b_out[0] = kk * a
    g_out[0] = g
    bonus_out[0] = bonus


def _rwkv_prep(hB, prev, prm, tb):
    B, T, _ = hB.shape
    nh = tb // SUBLANES
    row = lambda n: pl.BlockSpec((1, n), lambda b, i: (0, 0))
    full = lambda a: pl.BlockSpec(a.shape, lambda b, i: (0, 0))
    tok = pl.BlockSpec((1, tb, B_WIDTH), lambda b, i: (b, i, 0))
    return pl.pallas_call(
        _rwkv_prep_kernel,
        grid=(B, T // tb),
        in_specs=[
            pl.BlockSpec((1, tb, B_PAD), lambda b, i: (b, i, 0)),
            pl.BlockSpec((1, SUBLANES, B_PAD), lambda b, i: (b, jnp.maximum(i * nh - 1, 0), 0)),
            pl.BlockSpec((1, 1, B_PAD), lambda b, i: (b, 0, 0)),
            row(B_PAD), row(B_WIDTH), full(prm['ww']), row(B_WIDTH), full(prm['wa']), full(prm['wg']),
            row(B_WIDTH), row(B_WIDTH), row(B_WIDTH), full(prm['bd']),
        ],
        out_specs=[tok] * 8,
        out_shape=[jax.ShapeDtypeStruct((B, T, B_WIDTH), F32)] * 8,
        compiler_params=_cparams(("parallel", "parallel")),
        name="rwkv_prep",
    )(hB, hB, prev, prm['mu'], prm['w0'], prm['ww'], prm['a0'], prm['wa'], prm['wg'],
      prm['kk'], prm['ka'], prm['rk'], prm['bd'])


def _rwkv_scan_kernel(q_ref, d_ref, b_ref, k_ref, r_ref, v_ref, h0_ref, o_ref, hT_ref, h_scr, *, Tc):
    c = pl.program_id(1)
    JH = B_HEAD_DIM // 2

    @pl.when(c == 0)
    def _():
        h_scr[...] = h0_ref[0]

    def token(t, carry):
        u = jnp.zeros((B_HEAD_DIM, LANES), F32)
        for j in range(JH):
            u = u + q_ref[0, t, j:j + 1, :] * h_scr[j]
        u = u + pltpu.roll(u, LANES // 2, 1)
        vt = v_ref[0, t]
        o = jnp.zeros((B_HEAD_DIM, LANES), F32)
        for j in range(JH):
            hn = (h_scr[j] * d_ref[0, t, j:j + 1, :] + u * b_ref[0, t, j:j + 1, :]
                  + vt * k_ref[0, t, j:j + 1, :])
            h_scr[j] = hn
            o = o + hn * r_ref[0, t, j:j + 1, :]
        o_ref[0, t] = o + pltpu.roll(o, LANES // 2, 1)
        return carry

    lax.fori_loop(0, Tc, token, 0)

    @pl.when(c == pl.num_programs(1) - 1)
    def _():
        hT_ref[0] = h_scr[...]


def _rwkv_scan(q, d, b, k, r, v, h0, Tc):
    G, T = q.shape[:2]
    JH = B_HEAD_DIM // 2
    jspec = pl.BlockSpec((1, Tc, JH, LANES), lambda g, c: (g, c, 0, 0))
    ispec = pl.BlockSpec((1, Tc, B_HEAD_DIM, LANES), lambda g, c: (g, c, 0, 0))
    hspec = pl.BlockSpec((1, JH, B_HEAD_DIM, LANES), lambda g, c: (g, 0, 0, 0))
    return pl.pallas_call(
        functools.partial(_rwkv_scan_kernel, Tc=Tc),
        grid=(G, T // Tc),
        in_specs=[jspec] * 5 + [ispec, hspec],
        out_specs=[ispec, hspec],
        out_shape=[jax.ShapeDtypeStruct((G, T, B_HEAD_DIM, LANES), F32),
                   jax.ShapeDtypeStruct((G, JH, B_HEAD_DIM, LANES), F32)],
        scratch_shapes=[pltpu.VMEM((JH, B_HEAD_DIM, LANES), F32)],
        compiler_params=_cparams(("parallel", "arbitrary")),
        name="rwkv_scan",
    )(q, d, b, k, r, v, h0)


def _to_scan_j(x, B, T):
    G = B // 4
    x = x.reshape(G, 4, T, B_HEADS, 2, B_HEAD_DIM // 2)
    return x.transpose(0, 2, 5, 4, 1, 3).reshape(G, T, B_HEAD_DIM // 2, LANES)


def _to_scan_i(x, B, T):
    G = B // 4
    x = x.reshape(G, 4, T, B_HEADS, B_HEAD_DIM).transpose(0, 2, 4, 1, 3).reshape(G, T, B_HEAD_DIM, 64)
    return jnp.concatenate([x, x], axis=-1)


def _from_scan_i(o, B, T):
    G = B // 4
    o = o[..., :64].reshape(G, T, B_HEAD_DIM, 4, B_HEADS).transpose(0, 3, 1, 4, 2)
    return o.reshape(B, T, B_WIDTH)


def _state_to_scan(S, B):
    G = B // 4
    S = S.reshape(G, 4, B_HEADS, B_HEAD_DIM, 2, B_HEAD_DIM // 2)
    return S.transpose(0, 5, 3, 4, 1, 2).reshape(G, B_HEAD_DIM // 2, B_HEAD_DIM, LANES)


def _state_from_scan(Hs, B):
    G = B // 4
    Hs = Hs.reshape(G, B_HEAD_DIM // 2, B_HEAD_DIM, 2, 4, B_HEADS)
    return Hs.transpose(0, 4, 5, 2, 3, 1).reshape(B, B_HEADS, B_HEAD_DIM, B_HEAD_DIM)


def _layer_norm(y, g, b):
    mu = jnp.mean(y, axis=-1, keepdims=True)
    var = jnp.mean(jnp.square(y - mu), axis=-1, keepdims=True)
    return (y - mu) * lax.rsqrt(var + LN_EPS) * g + b


def _outproj_kernel(oa_ref, o_ref, bonus_ref, g_ref, x_ref, wa_ref, wb_ref, gng_ref, gnb_ref,
                    lng_ref, lnb_ref, bd_ref, y_ref):
    bd = bd_ref[...]
    o = o_ref[...]
    inv = 1.0 / B_HEAD_DIM
    om = _head_sum(o, bd) * inv
    oc = o - om
    ov = _head_sum(oc * oc, bd) * inv
    ob = (oc * lax.rsqrt(ov + GN_EPS) * gng_ref[...] + gnb_ref[...] + bonus_ref[...]) * g_ref[...]
    mix = (jnp.dot(oa_ref[...].astype(BF16), wa_ref[...], preferred_element_type=F32)
           + jnp.dot(ob.astype(BF16), wb_ref[...], preferred_element_type=F32))
    y_ref[...] = _layer_norm(ALPHA * x_ref[...] + mix, lng_ref[...], lnb_ref[...])


def _outproj(oA, o, bonus, g, x, prm, tm):
    N = x.shape[0]
    tokA = pl.BlockSpec((tm, A_WIDTH), lambda i: (i, 0))
    tokD = pl.BlockSpec((tm, D_MODEL), lambda i: (i, 0))
    full = lambda a: pl.BlockSpec(a.shape, lambda i: (0, 0))
    return pl.pallas_call(
        _outproj_kernel,
        grid=(N // tm,),
        in_specs=[tokA, tokA, tokA, tokA, tokD, full(prm['wo_a']), full(prm['wo_b']),
                  full(prm['gn_g']), full(prm['gn_b']), full(prm['ln1_g']), full(prm['ln1_b']),
                  full(prm['bd'])],
        out_specs=tokD,
        out_shape=jax.ShapeDtypeStruct((N, D_MODEL), F32),
        compiler_params=_cparams(("parallel",)),
        name="out_proj_ln",
    )(oA, o, bonus, g, x, prm['wo_a'], prm['wo_b'], prm['gn_g'], prm['gn_b'],
      prm['ln1_g'], prm['ln1_b'], prm['bd'])


def _router_kernel(x_ref, rwh_ref, rwl_ref, rb_ref, tri_ref, gate_ref, esel_ref, rank_ref, cnt_ref, oh_scr):
    xh, xl = _split(x_ref[...])
    nt = lambda a, b: lax.dot_general(a, b, (((1,), (1,)), ((), ())), preferred_element_type=F32)
    lg = nt(rwh_ref[...], xh) + nt(rwh_ref[...], xl) + nt(rwl_ref[...], xh)
    sc = jax.nn.sigmoid(lg)
    bi = sc + rb_ref[...]
    G, M = N_GROUPS, EXPERTS_PER_GROUP
    X = [bi[m * G:(m + 1) * G] for m in range(M)]
    S = [sc[m * G:(m + 1) * G] for m in range(M)]
    gs = None
    for a in range(M):
        for b in range(a + 1, M):
            pr = X[a] + X[b]
            gs = pr if gs is None else jnp.maximum(gs, pr)
    best = gs[0:1]
    gsel = jnp.zeros_like(best, dtype=jnp.int32)
    for g in range(1, G):
        c = gs[g:g + 1] > best
        gsel = jnp.where(c, g, gsel)
        best = jnp.where(c, gs[g:g + 1], best)
    vals, raws = [], []
    for m in range(M):
        vm, sm = X[m][0:1], S[m][0:1]
        for g in range(1, G):
            vm = jnp.where(gsel == g, X[m][g:g + 1], vm)
            sm = jnp.where(gsel == g, S[m][g:g + 1], sm)
        vals.append(vm)
        raws.append(sm)

    def first_argmax(vs):
        bv, bi_ = vs[0], jnp.zeros_like(gsel)
        for m in range(1, M):
            c = vs[m] > bv
            bi_ = jnp.where(c, m, bi_)
            bv = jnp.where(c, vs[m], bv)
        return bi_

    i1 = first_argmax(vals)
    i2 = first_argmax([jnp.where(i1 == m, -jnp.inf, vals[m]) for m in range(M)])
    w1 = sum(jnp.where(i1 == m, raws[m], 0.0) for m in range(M))
    w2 = sum(jnp.where(i2 == m, raws[m], 0.0) for m in range(M))
    den = w1 + w2
    gate_ref[0:1, :] = w1 / den
    gate_ref[1:2, :] = w2 / den
    esel_ref[0:1, :] = gsel * M + i1
    esel_ref[1:2, :] = gsel * M + i2

    @pl.when(pl.program_id(0) == 0)
    def _():
        cnt_ref[...] = jnp.zeros(cnt_ref.shape, F32)

    for g in range(G):
        for m in range(M):
            hit = ((gsel == g) & (i1 == m)) | ((gsel == g) & (i2 == m))
            oh_scr[g * M + m:g * M + m + 1, :] = jnp.where(hit, 1.0, 0.0)
    oh = oh_scr[...]
    before = cnt_ref[...] + jnp.dot(oh.astype(BF16), tri_ref[...], preferred_element_type=F32)
    r1 = jnp.zeros_like(w1)
    r2 = jnp.zeros_like(w1)
    for g in range(G):
        for m in range(M):
            e = g * M + m
            r1 = jnp.where((gsel == g) & (i1 == m), before[e:e + 1], r1)
            r2 = jnp.where((gsel == g) & (i2 == m), before[e:e + 1], r2)
    rank_ref[0:1, :] = r1.astype(jnp.int32)
    rank_ref[1:2, :] = r2.astype(jnp.int32)
    cnt_ref[...] += jnp.sum(oh, axis=1, keepdims=True)


def _router(x, rw, rb, tm):
    N = x.shape[0]
    rwh, rwl = rw
    tri = jnp.asarray(np.arange(tm)[:, None] < np.arange(tm)[None, :], BF16)
    slot = pl.BlockSpec((2, tm), lambda i: (0, i))
    return pl.pallas_call(
        _router_kernel,
        grid=(N // tm,),
        in_specs=[pl.BlockSpec((tm, D_MODEL), lambda i: (i, 0)),
                  pl.BlockSpec(rwh.shape, lambda i: (0, 0)),
                  pl.BlockSpec(rwl.shape, lambda i: (0, 0)),
                  pl.BlockSpec(rb.shape, lambda i: (0, 0)),
                  pl.BlockSpec(tri.shape, lambda i: (0, 0))],
        out_specs=[slot, slot, slot, pl.BlockSpec((N_EXPERTS, 1), lambda i: (0, 0))],
        out_shape=[jax.ShapeDtypeStruct((2, N), F32), jax.ShapeDtypeStruct((2, N), jnp.int32),
                   jax.ShapeDtypeStruct((2, N), jnp.int32), jax.ShapeDtypeStruct((N_EXPERTS, 1), F32)],
        scratch_shapes=[pltpu.VMEM((N_EXPERTS, tm), F32)],
        compiler_params=_cparams(("arbitrary",)),
        name="router",
    )(x, rwh, rwl, rb, tri)


def _row_copy(src, src_row, dst, dst_row, sem):
    return pltpu.make_async_copy(src.at[pl.ds(src_row, 1)], dst.at[pl.ds(dst_row, 1)], sem)


def _dispatch_kernel(dest_ref, x_ref, xs_in_ref, xs_ref, sem, *, N, tb):
    del xs_in_ref
    base = pl.program_id(0) * tb

    def issue(t, c):
        for s in range(2):
            _row_copy(x_ref, t, xs_ref, dest_ref[s * N + base + t], sem.at[0]).start()
        return c

    lax.fori_loop(0, tb, issue, 0, unroll=8)

    def drain(t, c):
        for s in range(2):
            _row_copy(x_ref, 0, xs_ref, 0, sem.at[0]).wait()
        return c

    lax.fori_loop(0, tb, drain, 0, unroll=8)


def _dispatch(dest, x, rows, tb):
    N = x.shape[0]
    return pl.pallas_call(
        functools.partial(_dispatch_kernel, N=N, tb=tb),
        grid_spec=pltpu.PrefetchScalarGridSpec(
            num_scalar_prefetch=1,
            grid=(N // tb,),
            in_specs=[pl.BlockSpec((tb, D_MODEL), lambda i, d: (i, 0)),
                      pl.BlockSpec(memory_space=pl.ANY)],
            out_specs=pl.BlockSpec(memory_space=pl.ANY),
            scratch_shapes=[pltpu.SemaphoreType.DMA((1,))],
        ),
        out_shape=jax.ShapeDtypeStruct((rows, D_MODEL), F32),
        input_output_aliases={2: 0},
        compiler_params=_cparams(("arbitrary",)),
        name="moe_dispatch",
    )(dest, x, jnp.zeros((rows, D_MODEL), F32))


def _experts_kernel(te_ref, tv_ref, xs_ref, w1_ref, w3_ref, w2_ref, y_ref):
    del te_ref

    @pl.when(tv_ref[pl.program_id(0)] != 0)
    def _():
        xb = xs_ref[...].astype(BF16)
        h1 = jnp.dot(xb, w1_ref[0], preferred_element_type=F32)
        h3 = jnp.dot(xb, w3_ref[0], preferred_element_type=F32)
        hh = (h1 * jax.nn.sigmoid(h1) * h3).astype(BF16)
        y_ref[...] = jnp.dot(hh, w2_ref[0], preferred_element_type=F32)

    @pl.when(tv_ref[pl.program_id(0)] == 0)
    def _():
        y_ref[...] = jnp.zeros(y_ref.shape, F32)


def _experts(tile_expert, tile_valid, xs, w1, w3, w2, tm):
    rows = xs.shape[0]
    wspec = lambda shp: pl.BlockSpec((1,) + shp, lambda i, te, tv: (te[i], 0, 0))
    return pl.pallas_call(
        _experts_kernel,
        grid_spec=pltpu.PrefetchScalarGridSpec(
            num_scalar_prefetch=2,
            grid=(rows // tm,),
            in_specs=[pl.BlockSpec((tm, D_MODEL), lambda i, te, tv: (i, 0)),
                      wspec((D_MODEL, D_EXPERT)), wspec((D_MODEL, D_EXPERT)), wspec((D_EXPERT, D_MODEL))],
            out_specs=pl.BlockSpec((tm, D_MODEL), lambda i, te, tv: (i, 0)),
        ),
        out_shape=jax.ShapeDtypeStruct((rows, D_MODEL), F32),
        compiler_params=_cparams(("arbitrary",)),
        name="moe_experts",
    )(tile_expert, tile_valid, xs, w1, w3, w2)


def _combine_kernel(dest_ref, x_ref, gate_ref, ys_ref, lng_ref, lnb_ref, o_ref, ybuf, sem, *, N, tb):
    base = pl.program_id(0) * tb

    def issue(t, c):
        for s in range(2):
            _row_copy(ys_ref, dest_ref[s * N + base + t], ybuf.at[s], t, sem.at[0]).start()
        return c

    lax.fori_loop(0, tb, issue, 0, unroll=8)

    def drain(t, c):
        for s in range(2):
            _row_copy(ys_ref, 0, ybuf.at[s], 0, sem.at[0]).wait()
        return c

    lax.fori_loop(0, tb, drain, 0, unroll=8)
    gate = gate_ref[...]
    f = gate[:, 0:1] * ybuf[0] + gate[:, 1:2] * ybuf[1]
    o_ref[...] = _layer_norm(ALPHA * x_ref[...] + f, lng_ref[...], lnb_ref[...])


def _combine(dest, x, gate_t, ys, lng, lnb, tb):
    N = x.shape[0]
    full = lambda a: pl.BlockSpec(a.shape, lambda i, d: (0, 0))
    return pl.pallas_call(
        functools.partial(_combine_kernel, N=N, tb=tb),
        grid_spec=pltpu.PrefetchScalarGridSpec(
            num_scalar_prefetch=1,
            grid=(N // tb,),
            in_specs=[pl.BlockSpec((tb, D_MODEL), lambda i, d: (i, 0)),
                      pl.BlockSpec((tb, 2), lambda i, d: (i, 0)),
                      pl.BlockSpec(memory_space=pl.ANY), full(lng), full(lnb)],
            out_specs=pl.BlockSpec((tb, D_MODEL), lambda i, d: (i, 0)),
            scratch_shapes=[pltpu.VMEM((2, tb, D_MODEL), F32), pltpu.SemaphoreType.DMA((1,))],
        ),
        out_shape=jax.ShapeDtypeStruct((N, D_MODEL), F32),
        compiler_params=_cparams(("arbitrary",)),
        name="moe_combine_ln",
    )(dest, x, gate_t, ys, lng, lnb)


def _moe(x, rw, rb, w1, w3, w2, lng, lnb):
    N = x.shape[0]
    tm = min(512, N)
    gate, esel, rank, counts = _router(x, rw, rb, tm)
    n_tiles = 2 * N // tm + N_EXPERTS
    cnt = counts[:, 0].astype(jnp.int32)
    padded = (cnt + tm - 1) // tm * tm
    ends = jnp.cumsum(padded)
    dest = ((ends - padded)[esel] + rank).reshape(-1)
    tile_start = jnp.arange(n_tiles, dtype=jnp.int32) * tm
    tile_valid = (tile_start < ends[-1]).astype(jnp.int32)
    last_start = jnp.maximum(ends[-1] - tm, 0)
    tile_expert = jnp.sum(jnp.minimum(tile_start, last_start)[:, None] >= ends[None, :], axis=1).astype(jnp.int32)
    xs = _dispatch(dest, x, n_tiles * tm, min(256, N))
    ys = _experts(tile_expert, tile_valid, xs, w1, w3, w2, tm)
    return _combine(dest, x, gate.T, ys, lng, lnb, min(256, N))


def _row(v, n=None):
    v = v.reshape(1, -1).astype(F32)
    if n is not None and v.shape[1] < n:
        v = jnp.pad(v, ((0, 0), (0, n - v.shape[1])))
    return v


def _layer_params(l, w_in, w_out, ln1_g, ln1_b, ln2_g, ln2_b, rwkv_mu, rwkv_w0, rwkv_w2, rwkv_a0,
                  rwkv_a2, rwkv_g2, rwkv_kk, rwkv_ka, rwkv_rk, rwkv_gn_g, rwkv_gn_b,
                  exp_w1, exp_w3, exp_w2):
    wi = w_in[l]
    offs = np.cumsum((0, A_WIDTH, A_KV_WIDTH, A_KV_WIDTH, IDX_WIDTH, IDX_DIM, IDX_HEADS))
    q, k, v, qi, ki, wgt = (wi[:, offs[n]:offs[n + 1]] for n in range(6))
    w_a1 = jnp.concatenate([q, k, v], axis=1)
    w_a2 = jnp.concatenate([qi, ki, wgt, jnp.zeros((D_MODEL, LANES - IDX_DIM - IDX_HEADS), F32)], axis=1)
    w_a2h, w_a2l = _split(w_a2)
    w_b = jnp.pad(wi[:, A_PROJ:], ((0, 0), (0, B_PAD - B_PROJ)))
    zl = jnp.zeros((DECAY_LORA, B_WIDTH), F32)
    seg = np.arange(LANES) // B_HEAD_DIM
    return dict(
        w_a1=w_a1.astype(BF16), w_a2h=w_a2h, w_a2l=w_a2l, w_b=w_b.astype(BF16),
        wo_a=w_out[l][:A_WIDTH].astype(BF16), wo_b=w_out[l][A_WIDTH:].astype(BF16),
        ln1_g=_row(ln1_g[l]), ln1_b=_row(ln1_b[l]), ln2_g=_row(ln2_g[l]), ln2_b=_row(ln2_b[l]),
        mu=_row(rwkv_mu[l], B_PAD), w0=_row(rwkv_w0[l]), a0=_row(rwkv_a0[l]),
        ww=jnp.concatenate([rwkv_w2[l], zl], axis=0).astype(BF16),
        wa=jnp.concatenate([zl, rwkv_a2[l]], axis=0).astype(BF16),
        wg=jnp.pad(rwkv_g2[l], ((0, 2 * LANES - GATE_LORA), (0, 0))).astype(BF16),
        kk=_row(rwkv_kk[l]), ka=_row(rwkv_ka[l]), rk=_row(rwkv_rk[l]),
        gn_g=_row(rwkv_gn_g[l]), gn_b=_row(rwkv_gn_b[l]),
        bd=jnp.asarray(seg[:, None] == seg[None, :], BF16),
        w1=exp_w1[l].astype(BF16), w3=exp_w3[l].astype(BF16), w2=exp_w2[l].astype(BF16),
    )


def _trunk_layer(x, prm, rw, rb, past):
    B, T, _ = x.shape
    N = B * T
    xt = x.reshape(N, D_MODEL)
    tm = min(512, N)
    hA1 = _matmul(xt, prm['w_a1'], tm, A1_PAD // 2).reshape(B, T, A1_PAD)
    hA2 = _matmul(xt, prm['w_a2h'], tm, A2_PAD // 3, prm['w_a2l']).reshape(B, T, A2_PAD)
    hB = _matmul(xt, prm['w_b'], tm, B_PAD // 3).reshape(B, T, B_PAD)
    k_new = hA1[..., A1_OFF_K:A1_OFF_K + A_KV_WIDTH]
    v_new = hA1[..., A1_OFF_V:A1_OFF_V + A_KV_WIDTH]
    ki_new = hA2[..., A2_OFF_KW:A2_OFF_KW + IDX_DIM]

    if past is None:
        kf, vf, kif = k_new, v_new, ki_new
        pos0, TQ = 0, min(Q_BLOCK, T)
        S0 = jnp.zeros((B, B_HEADS, B_HEAD_DIM, B_HEAD_DIM), F32)
        prev = jnp.zeros((B, 1, B_PAD), F32)
    else:
        ck, cv, cki, S0, prev = past
        P = ck.shape[1]
        kf = jnp.concatenate([ck.reshape(B, P, A_KV_WIDTH), k_new], axis=1)
        vf = jnp.concatenate([cv.reshape(B, P, A_KV_WIDTH), v_new], axis=1)
        kif = jnp.concatenate([cki, ki_new], axis=1)
        pos0, TQ = P, T
        prev = jnp.pad(prev, ((0, 0), (0, 0), (0, B_PAD - B_PROJ)))
    L = kf.shape[1]
    ksel = min(TOPK_MAX, L // 4)
    TK = min(512, -(-L // LANES) * LANES)
    Lp = -(-L // TK) * TK
    padk = lambda a: jnp.pad(a.astype(BF16), ((0, 0), (0, Lp - L), (0, 0)))
    ki_hi, ki_lo = _split(kif.astype(F32))
    ki3 = jnp.concatenate([ki_hi, ki_hi, ki_lo, jnp.zeros_like(ki_hi)], axis=-1)
    oA = _dsa(hA1, hA2, padk(kf), padk(vf), padk(ki3), TQ=TQ, L=L, pos0=pos0, ksel=ksel)

    r, d, k, v, q, b, g, bonus = _rwkv_prep(hB, prev, prm, min(256, T))
    o_s, hT = _rwkv_scan(_to_scan_j(q, B, T), _to_scan_j(d, B, T), _to_scan_j(b, B, T),
                         _to_scan_j(k, B, T), _to_scan_j(r, B, T), _to_scan_i(v, B, T),
                         _state_to_scan(S0.astype(F32), B), min(32, T))
    o = _from_scan_i(o_s, B, T)
    S_T = _state_from_scan(hT, B)
    last = hB[:, -1:, :B_PROJ]

    tmo = min(256, N)
    x1 = _outproj(oA.reshape(N, A_WIDTH), o.reshape(N, B_WIDTH), bonus.reshape(N, B_WIDTH),
                  g.reshape(N, B_WIDTH), xt, prm, tmo)
    x2 = _moe(x1, rw, rb, prm['w1'], prm['w3'], prm['w2'], prm['ln2_g'], prm['ln2_b'])
    state = (k_new.reshape(B, T, A_KV_HEADS, A_HEAD_DIM), v_new.reshape(B, T, A_KV_HEADS, A_HEAD_DIM),
             ki_new, S_T, last)
    return x2.reshape(B, T, D_MODEL), state


def kernel(x_prompt, x_sample, cache_k, cache_v, cache_kidx, state_wkv, state_shift, w_in, w_out, ln1_g, ln1_b, ln2_g, ln2_b, rwkv_mu, rwkv_w0, rwkv_w2, rwkv_a0, rwkv_a2, rwkv_g2, rwkv_kk, rwkv_ka, rwkv_rk, rwkv_gn_g, rwkv_gn_b, router_w, router_b, exp_w1, exp_w3, exp_w2):
    assert x_prompt.shape[0] % 4 == 0 and x_sample.shape[0] % 4 == 0
    perm = np.array([g * EXPERTS_PER_GROUP + m for m in range(EXPERTS_PER_GROUP) for g in range(N_GROUPS)])
    rw = _split(router_w.T[perm].astype(F32))
    rb = router_b[perm].reshape(N_EXPERTS, 1).astype(F32)
    yp, ys = x_prompt, x_sample
    new_p, new_s = [], []
    for l in range(DEPTH):
        prm = _layer_params(l, w_in, w_out, ln1_g, ln1_b, ln2_g, ln2_b, rwkv_mu, rwkv_w0, rwkv_w2,
                            rwkv_a0, rwkv_a2, rwkv_g2, rwkv_kk, rwkv_ka, rwkv_rk, rwkv_gn_g, rwkv_gn_b,
                            exp_w1, exp_w3, exp_w2)
        yp, st_p = _trunk_layer(yp, prm, rw, rb, None)
        ys, st_s = _trunk_layer(ys, prm, rw, rb, (cache_k[l], cache_v[l], cache_kidx[l],
                                                   state_wkv[l], state_shift[l]))
        new_p.append(st_p)
        new_s.append(st_s)
    stack = lambda sts, n: jnp.stack([s[n] for s in sts])
    return (yp, ys) + tuple(stack(new_p, n) for n in range(5)) + tuple(stack(new_s, n) for n in range(5))
```

```python
import functools

import numpy as np
import jax
import jax.numpy as jnp
from jax import lax
from jax.experimental import pallas as pl
from jax.experimental.pallas import tpu as pltpu

F32 = jnp.float32
BF16 = jnp.bfloat16

D_MODEL = 2048
CHUNK = 64
Q_BLOCK = 128
A_HEADS = 8
A_KV_HEADS = 2
A_HEAD_DIM = 128
A_WIDTH = A_HEADS * A_HEAD_DIM
A_KV_WIDTH = A_KV_HEADS * A_HEAD_DIM
IDX_HEADS = 16
IDX_DIM = 64
IDX_WIDTH = IDX_HEADS * IDX_DIM
TOPK_MAX = 256
B_HEADS = 16
B_HEAD_DIM = 64
B_WIDTH = B_HEADS * B_HEAD_DIM
DECAY_LORA = 64
ICL_LORA = 64
GATE_LORA = 160
GN_EPS = 64e-5
N_EXPERTS = 16
N_GROUPS = 4
EXPERTS_PER_GROUP = 4
D_EXPERT = 1024
LN_EPS = 1e-5
DEPTH = 2
ALPHA = (2 * DEPTH) ** 0.25
A_PROJ = A_WIDTH + 2 * A_KV_WIDTH + IDX_WIDTH + IDX_DIM + IDX_HEADS
B_PROJ = 3 * B_WIDTH + DECAY_LORA + ICL_LORA + GATE_LORA

LANES = 128
SUBLANES = 8
VMEM_LIMIT = 56 * 1024 * 1024

A1_PAD = A_WIDTH + 2 * A_KV_WIDTH
A1_OFF_K, A1_OFF_V = A_WIDTH, A_WIDTH + A_KV_WIDTH
A2_PAD = IDX_WIDTH + LANES
A2_OFF_KW = IDX_WIDTH
B_PAD = 3 * B_WIDTH + LANES + 2 * LANES
B_OFF_LORA = 3 * B_WIDTH
B_OFF_GATE = B_OFF_LORA + LANES

INT_MIN = np.int32(-2 ** 31)
INT_MAX = np.int32(2 ** 31 - 1)
MASKED_DIST = -1e33
INIT_MAX = -1e30


def _cparams(sem):
    return pltpu.CompilerParams(dimension_semantics=sem, vmem_limit_bytes=VMEM_LIMIT)


def _mm_kernel(x_ref, w_ref, o_ref):
    o_ref[...] = jnp.dot(x_ref[...].astype(BF16), w_ref[...], preferred_element_type=F32)


def _split(x):
    hi = x.astype(BF16)
    return hi, (x - hi.astype(F32)).astype(BF16)


def _mm_split_kernel(x_ref, wh_ref, wl_ref, o_ref):
    hi, lo = _split(x_ref[...])
    wh = wh_ref[...]
    o_ref[...] = (jnp.dot(hi, wh, preferred_element_type=F32) + jnp.dot(lo, wh, preferred_element_type=F32)
                  + jnp.dot(hi, wl_ref[...], preferred_element_type=F32))


def _matmul(x, w, tm, tn, w_lo=None):
    M, K = x.shape
    N = w.shape[1]
    wspec = pl.BlockSpec((K, tn), lambda j, i: (0, j))
    ws = (w,) if w_lo is None else (w, w_lo)
    return pl.pallas_call(
        _mm_kernel if w_lo is None else _mm_split_kernel,
        grid=(N // tn, M // tm),
        in_specs=[pl.BlockSpec((tm, K), lambda j, i: (i, 0))] + [wspec] * len(ws),
        out_specs=pl.BlockSpec((tm, tn), lambda j, i: (i, j)),
        out_shape=jax.ShapeDtypeStruct((M, N), F32),
        compiler_params=_cparams(("parallel", "parallel")),
        name="in_proj" if w_lo is None else "in_proj_split",
    )(x, *ws)


def _dsa_kernel(q_ref, qi_ref, kw_ref, k_ref, v_ref, ki_ref, o_ref,
                key_scr, qim_scr, wb_scr, qs_scr, p_scr, m_scr, l_scr, acc_scr,
                *, TQ, TK, KC, L, pos0, ksel, idx_bits):
    i = pl.program_id(1)
    qpos0 = pos0 + i * TQ
    row = lax.broadcasted_iota(jnp.int32, (TQ, 1), 0)
    qpos = qpos0 + row
    limit = jnp.minimum((qpos // CHUNK + 1) * CHUNK, L)
    limit_max = jnp.minimum(((qpos0 + TQ - 1) // CHUNK + 1) * CHUNK, L)
    nkt = (limit_max + TK - 1) // TK
    lane = lax.broadcasted_iota(jnp.int32, (1, TK), 1)

    kw = kw_ref[0]
    half = lax.broadcasted_iota(jnp.int32, (1, LANES), 1) < IDX_DIM
    for p in range(IDX_HEADS // 2):
        slab = qi_ref[0, :, p * LANES:(p + 1) * LANES]
        hi = slab.astype(BF16).astype(F32)
        lo = slab - hi
        hi_sw = pltpu.roll(hi, IDX_DIM, 1)
        lo_sw = pltpu.roll(lo, IDX_DIM, 1)
        ra = slice(2 * p * TQ, (2 * p + 1) * TQ)
        rb = slice((2 * p + 1) * TQ, (2 * p + 2) * TQ)
        qim_scr[ra, 0:LANES] = jnp.where(half, hi, lo_sw).astype(BF16)
        qim_scr[ra, LANES:2 * LANES] = jnp.where(half, hi, 0.0).astype(BF16)
        qim_scr[rb, 0:LANES] = jnp.where(half, hi_sw, lo).astype(BF16)
        qim_scr[rb, LANES:2 * LANES] = jnp.where(half, hi_sw, 0.0).astype(BF16)
    for h in range(IDX_HEADS):
        wb_scr[h] = jnp.broadcast_to(kw[:, IDX_DIM + h:IDX_DIM + h + 1] * (IDX_WIDTH ** -0.5), (TQ, LANES))
    lane1 = lax.broadcasted_iota(jnp.int32, (1, LANES), 1)
    nt_dims = (((1,), (1,)), ((), ()))

    def score_tile(kt, carry):
        ks = pl.multiple_of(kt * TK, TK)
        for c in range(TK // KC):
            k0 = pl.multiple_of(ks + c * KC, KC)
            ki_t = ki_ref[0, pl.ds(k0, KC), :]
            accs = [jnp.zeros((TQ, LANES), F32) for _ in range(KC // LANES)]
            d_all = lax.dot_general(qim_scr[...], ki_t, nt_dims, preferred_element_type=F32)
            for h in range(IDX_HEADS):
                d = d_all[h * TQ:(h + 1) * TQ]
                wb = wb_scr[h]
                for s in range(KC // LANES):
                    accs[s] = accs[s] + jnp.maximum(d[:, s * LANES:(s + 1) * LANES], 0.0) * wb
            for s in range(KC // LANES):
                acc = accs[s]
                bits = pltpu.bitcast(acc, jnp.int32)
                key = bits ^ ((bits >> 31) & INT_MAX)
                key = jnp.where(acc == 0.0, 0, key)
                kpos = k0 + s * LANES + lane1
                key_scr[:, pl.ds(pl.multiple_of(k0 + s * LANES, LANES), LANES)] = (
                    jnp.where(kpos < limit, key, INT_MIN))
        return carry

    lax.fori_loop(0, nkt, score_tile, 0)

    def count(pred_fn):
        def body(kt, cnt):
            ks = pl.multiple_of(kt * TK, TK)
            m = pred_fn(key_scr[:, pl.ds(ks, TK)], ks)
            c = jnp.where(m, 1.0, 0.0)
            for s in range(TK // LANES):
                cnt = cnt + c[:, s * LANES:(s + 1) * LANES]
            return cnt
        cnt = lax.fori_loop(0, nkt, body, jnp.zeros((TQ, LANES), F32))
        return jnp.sum(cnt, axis=1, keepdims=True)

    kself = float(ksel)

    def bit_step(b, thr):
        cand = thr + jnp.left_shift(jnp.int32(1), 31 - b)
        c = count(lambda key, ks: key >= cand)
        return jnp.where(c >= kself, cand, thr)

    thr = lax.fori_loop(0, 32, bit_step, jnp.full((TQ, 1), INT_MIN, jnp.int32))
    n_gt = count(lambda key, ks: key > thr)
    n_ge = count(lambda key, ks: key >= thr)
    need = kself - n_gt

    def idx_step(b, jm):
        cand = jm + jnp.left_shift(jnp.int32(1), idx_bits - 1 - b)
        c = count(lambda key, ks: (key == thr) & ((ks + lane) < cand))
        return jnp.where(c < need, cand, jm)

    excess = n_ge > kself
    jm = lax.cond(jnp.max(jnp.where(excess, 1.0, 0.0)) > 0.0,
                  lambda: lax.fori_loop(0, idx_bits, idx_step, jnp.zeros((TQ, 1), jnp.int32)),
                  lambda: jnp.zeros((TQ, 1), jnp.int32))
    jm = jnp.where(excess, jm, INT_MAX)

    grp = A_HEADS // A_KV_HEADS
    log2e = float(np.log2(np.e))
    qscale = A_HEAD_DIM ** -0.5 * log2e
    for n in range(A_KV_HEADS):
        for g in range(grp):
            hh = n * grp + g
            qs_scr[n, g * TQ:(g + 1) * TQ, :] = (
                q_ref[0, :, hh * A_HEAD_DIM:(hh + 1) * A_HEAD_DIM] * qscale).astype(BF16)
    m_scr[...] = jnp.full(m_scr.shape, INIT_MAX, F32)
    l_scr[...] = jnp.zeros(l_scr.shape, F32)
    acc_scr[...] = jnp.zeros(acc_scr.shape, F32)
    slopes2 = [float(2.0 ** (-8.0 * (h + 1.0) / A_HEADS)) * log2e for h in range(A_HEADS)]
    lane_kc = lax.broadcasted_iota(jnp.int32, (1, KC), 1)

    def attn_tile(kt, carry):
        ks = pl.multiple_of(kt * TK, TK)
        for c in range(TK // KC):
            k0 = pl.multiple_of(ks + c * KC, KC)
            key = key_scr[:, pl.ds(k0, KC)]
            kpos = k0 + lane_kc
            sel = ((key > thr) | ((key == thr) & (kpos <= jm))) & (kpos < limit)
            ndm = jnp.where(sel, -jnp.abs(qpos - kpos).astype(F32), MASKED_DIST)
            for n in range(A_KV_HEADS):
                k_t = k_ref[0, pl.ds(k0, KC), n * A_HEAD_DIM:(n + 1) * A_HEAD_DIM]
                v_t = v_ref[0, pl.ds(k0, KC), n * A_HEAD_DIM:(n + 1) * A_HEAD_DIM]
                s_all = lax.dot_general(qs_scr[n], k_t, nt_dims, preferred_element_type=F32)
                for g in range(grp):
                    hh = n * grp + g
                    lg = s_all[g * TQ:(g + 1) * TQ] + slopes2[hh] * ndm
                    m_prev = m_scr[hh]
                    mx = lg[:, 0:LANES]
                    for s in range(1, KC // LANES):
                        mx = jnp.maximum(mx, lg[:, s * LANES:(s + 1) * LANES])
                    m_new = jnp.maximum(m_prev, jnp.max(mx, axis=1, keepdims=True))
                    alpha = jnp.exp2(m_prev - m_new)
                    ps = [jnp.exp2(lg[:, s * LANES:(s + 1) * LANES] - m_new) for s in range(KC // LANES)]
                    l_new = alpha * l_scr[hh]
                    for pp in ps:
                        l_new = l_new + pp
                    l_scr[hh] = l_new
                    p_scr[g * TQ:(g + 1) * TQ, :] = jnp.concatenate(ps, axis=1).astype(BF16)
                    acc_scr[hh] = alpha * acc_scr[hh]
                    m_scr[hh] = m_new
                pv = jnp.dot(p_scr[...], v_t, preferred_element_type=F32)
                for g in range(grp):
                    acc_scr[n * grp + g] += pv[g * TQ:(g + 1) * TQ]
        return carry

    lax.fori_loop(0, nkt, attn_tile, 0)
    for hh in range(A_HEADS):
        o_ref[0, :, hh * A_HEAD_DIM:(hh + 1) * A_HEAD_DIM] = (
            acc_scr[hh] / jnp.sum(l_scr[hh], axis=1, keepdims=True))


def _dsa(hA1, hA2, kb, vb, kib, *, TQ, L, pos0, ksel):
    B, T, _ = hA1.shape
    Lp = kb.shape[1]
    TK = min(512, Lp)
    assert Lp % TK == 0 and TK >= ksel and T % TQ == 0
    idx_bits = int(Lp).bit_length()
    KC = min(2 * LANES, TK)
    kern = functools.partial(_dsa_kernel, TQ=TQ, TK=TK, KC=KC, L=L, pos0=pos0, ksel=ksel, idx_bits=idx_bits)
    return pl.pallas_call(
        kern,
        grid=(B, T // TQ),
        in_specs=[
            pl.BlockSpec((1, TQ, A_WIDTH), lambda b, i: (b, i, 0)),
            pl.BlockSpec((1, TQ, IDX_WIDTH), lambda b, i: (b, i, 0)),
            pl.BlockSpec((1, TQ, LANES), lambda b, i: (b, i, A2_OFF_KW // LANES)),
            pl.BlockSpec((1, Lp, A_KV_WIDTH), lambda b, i: (b, 0, 0)),
            pl.BlockSpec((1, Lp, A_KV_WIDTH), lambda b, i: (b, 0, 0)),
            pl.BlockSpec((1, Lp, 2 * LANES), lambda b, i: (b, 0, 0)),
        ],
        out_specs=pl.BlockSpec((1, TQ, A_WIDTH), lambda b, i: (b, i, 0)),
        out_shape=jax.ShapeDtypeStruct((B, T, A_WIDTH), F32),
        scratch_shapes=[
            pltpu.VMEM((TQ, Lp), jnp.int32),
            pltpu.VMEM((IDX_HEADS * TQ, 2 * LANES), BF16),
            pltpu.VMEM((IDX_HEADS, TQ, LANES), F32),
            pltpu.VMEM((A_KV_HEADS, (A_HEADS // A_KV_HEADS) * TQ, A_HEAD_DIM), BF16),
            pltpu.VMEM(((A_HEADS // A_KV_HEADS) * TQ, KC), BF16),
            pltpu.VMEM((A_HEADS, TQ, LANES), F32),
            pltpu.VMEM((A_HEADS, TQ, LANES), F32),
            pltpu.VMEM((A_HEADS, TQ, A_HEAD_DIM), F32),
        ],
        compiler_params=_cparams(("parallel", "arbitrary")),
        name="dsa",
    )(hA1, hA2, hA2, kb, vb, kib)


def _head_sum(x, bd):
    hi = x.astype(BF16)
    lo = (x - hi.astype(F32)).astype(BF16)
    outs = []
    for s in range(x.shape[1] // LANES):
        sl = slice(s * LANES, (s + 1) * LANES)
        outs.append(jnp.dot(hi[:, sl], bd, preferred_element_type=F32)
                    + jnp.dot(lo[:, sl], bd, preferred_element_type=F32))
    return jnp.concatenate(outs, axis=1)


def _rwkv_prep_kernel(hb_ref, halo_ref, prev_ref, mu_ref, w0_ref, ww_ref, a0_ref, wa_ref, wg_ref,
                      kk_ref, ka_ref, rk_ref, bd_ref,
                      r_out, d_out, k_out, v_out, q_out, b_out, g_out, bonus_out):
    i = pl.program_id(1)
    p = hb_ref[0]
    tb = p.shape[0]
    first_prev = jnp.where(i == 0, prev_ref[0], halo_ref[0, SUBLANES - 1:SUBLANES, :])
    rolled = pltpu.roll(p, 1, 0)
    rowid = lax.broadcasted_iota(jnp.int32, (tb, 1), 0)
    p_prev = jnp.where(rowid == 0, first_prev, rolled)
    ps = p + (p_prev - p) * mu_ref[...]
    r = ps[:, 0:B_WIDTH]
    k = ps[:, B_WIDTH:2 * B_WIDTH]
    v = ps[:, 2 * B_WIDTH:3 * B_WIDTH]
    lora = ps[:, B_OFF_LORA:B_OFF_LORA + LANES]
    gl = ps[:, B_OFF_GATE:B_OFF_GATE + 2 * LANES]
    wz = w0_ref[...] + jnp.dot(jnp.tanh(lora).astype(BF16), ww_ref[...], preferred_element_type=F32)
    w = -jax.nn.softplus(-wz) - 0.5
    decay = jnp.exp(-jnp.exp(w))
    a = jax.nn.sigmoid(a0_ref[...] + jnp.dot(lora.astype(BF16), wa_ref[...], preferred_element_type=F32))
    g = jnp.dot(jax.nn.sigmoid(gl).astype(BF16), wg_ref[...], preferred_element_type=F32)
    bd = bd_ref[...]
    kk = k * kk_ref[...]
    nrm = jnp.maximum(jnp.sqrt(_head_sum(kk * kk, bd)), 1e-12)
    kk = kk / nrm
    k_mod = k * (1.0 + (a - 1.0) * ka_ref[...])
    bonus = _head_sum(r * k_mod * rk_ref[...], bd) * v
    r_out[0] = r
    d_out[0] = decay
    k_out[0] = k_mod
    v_out[0] = v
    q_out[0] = -kk
    b_out[0] = kk * a
    g_out[0] = g
    bonus_out[0] = bonus


def _rwkv_prep(hB, prev, prm, tb):
    B, T, _ = hB.shape
    nh = tb // SUBLANES
    row = lambda n: pl.BlockSpec((1, n), lambda b, i: (0, 0))
    full = lambda a: pl.BlockSpec(a.shape, lambda b, i: (0, 0))
    tok = pl.BlockSpec((1, tb, B_WIDTH), lambda b, i: (b, i, 0))
    return pl.pallas_call(
        _rwkv_prep_kernel,
        grid=(B, T // tb),
        in_specs=[
            pl.BlockSpec((1, tb, B_PAD), lambda b, i: (b, i, 0)),
            pl.BlockSpec((1, SUBLANES, B_PAD), lambda b, i: (b, jnp.maximum(i * nh - 1, 0), 0)),
            pl.BlockSpec((1, 1, B_PAD), lambda b, i: (b, 0, 0)),
            row(B_PAD), row(B_WIDTH), full(prm['ww']), row(B_WIDTH), full(prm['wa']), full(prm['wg']),
            row(B_WIDTH), row(B_WIDTH), row(B_WIDTH), full(prm['bd']),
        ],
        out_specs=[tok] * 8,
        out_shape=[jax.ShapeDtypeStruct((B, T, B_WIDTH), F32)] * 8,
        compiler_params=_cparams(("parallel", "parallel")),
        name="rwkv_prep",
    )(hB, hB, prev, prm['mu'], prm['w0'], prm['ww'], prm['a0'], prm['wa'], prm['wg'],
      prm['kk'], prm['ka'], prm['rk'], prm['bd'])


def _rwkv_scan_kernel(q_ref, d_ref, b_ref, k_ref, r_ref, v_ref, h0_ref, o_ref, hT_ref, h_scr, *, Tc):
    c = pl.program_id(1)
    JH = B_HEAD_DIM // 2

    @pl.when(c == 0)
    def _():
        h_scr[...] = h0_ref[0]

    IH = B_HEAD_DIM // 2

    halves = [slice(ih * IH, (ih + 1) * IH) for ih in range(2)]

    def fold(x):
        return x + pltpu.roll(x, LANES // 2, 1)

    def first_u(rows):
        u = jnp.zeros((IH, LANES), F32)
        for j in range(JH):
            u = u + q_ref[0, 0, j:j + 1, :] * h_scr[j, rows, :]
        return fold(u)

    def token(t, us):
        tn = jnp.minimum(t + 1, Tc - 1)
        nxt = []
        for rows, u in zip(halves, us):
            vt = v_ref[0, t, rows, :]
            o = jnp.zeros((IH, LANES), F32)
            un = jnp.zeros((IH, LANES), F32)
            for j in range(JH):
                hn = (h_scr[j, rows, :] * d_ref[0, t, j:j + 1, :] + u * b_ref[0, t, j:j + 1, :]
                      + vt * k_ref[0, t, j:j + 1, :])
                h_scr[j, rows, :] = hn
                o = o + hn * r_ref[0, t, j:j + 1, :]
                un = un + hn * q_ref[0, tn, j:j + 1, :]
            o_ref[0, t, rows, :] = fold(o)
            nxt.append(fold(un))
        return tuple(nxt)

    lax.fori_loop(0, Tc, token, tuple(first_u(rows) for rows in halves))

    @pl.when(c == pl.num_programs(1) - 1)
    def _():
        hT_ref[0] = h_scr[...]


def _rwkv_scan(q, d, b, k, r, v, h0, Tc):
    G, T = q.shape[:2]
    JH = B_HEAD_DIM // 2
    jspec = pl.BlockSpec((1, Tc, JH, LANES), lambda g, c: (g, c, 0, 0))
    ispec = pl.BlockSpec((1, Tc, B_HEAD_DIM, LANES), lambda g, c: (g, c, 0, 0))
    hspec = pl.BlockSpec((1, JH, B_HEAD_DIM, LANES), lambda g, c: (g, 0, 0, 0))
    return pl.pallas_call(
        functools.partial(_rwkv_scan_kernel, Tc=Tc),
        grid=(G, T // Tc),
        in_specs=[jspec] * 5 + [ispec, hspec],
        out_specs=[ispec, hspec],
        out_shape=[jax.ShapeDtypeStruct((G, T, B_HEAD_DIM, LANES), F32),
                   jax.ShapeDtypeStruct((G, JH, B_HEAD_DIM, LANES), F32)],
        scratch_shapes=[pltpu.VMEM((JH, B_HEAD_DIM, LANES), F32)],
        compiler_params=_cparams(("parallel", "arbitrary")),
        name="rwkv_scan",
    )(q, d, b, k, r, v, h0)


def _to_scan_j(x, B, T):
    G = B // 4
    x = x.reshape(G, 4, T, B_HEADS, 2, B_HEAD_DIM // 2)
    return x.transpose(0, 2, 5, 4, 1, 3).reshape(G, T, B_HEAD_DIM // 2, LANES)


def _to_scan_i(x, B, T):
    G = B // 4
    x = x.reshape(G, 4, T, B_HEADS, B_HEAD_DIM).transpose(0, 2, 4, 1, 3).reshape(G, T, B_HEAD_DIM, 64)
    return jnp.concatenate([x, x], axis=-1)


def _from_scan_i(o, B, T):
    G = B // 4
    o = o[..., :64].reshape(G, T, B_HEAD_DIM, 4, B_HEADS).transpose(0, 3, 1, 4, 2)
    return o.reshape(B, T, B_WIDTH)


def _state_to_scan(S, B):
    G = B // 4
    S = S.reshape(G, 4, B_HEADS, B_HEAD_DIM, 2, B_HEAD_DIM // 2)
    return S.transpose(0, 5, 3, 4, 1, 2).reshape(G, B_HEAD_DIM // 2, B_HEAD_DIM, LANES)


def _state_from_scan(Hs, B):
    G = B // 4
    Hs = Hs.reshape(G, B_HEAD_DIM // 2, B_HEAD_DIM, 2, 4, B_HEADS)
    return Hs.transpose(0, 4, 5, 2, 3, 1).reshape(B, B_HEADS, B_HEAD_DIM, B_HEAD_DIM)


def _layer_norm(y, g, b):
    mu = jnp.mean(y, axis=-1, keepdims=True)
    var = jnp.mean(jnp.square(y - mu), axis=-1, keepdims=True)
    return (y - mu) * lax.rsqrt(var + LN_EPS) * g + b


def _outproj_kernel(oa_ref, o_ref, bonus_ref, g_ref, x_ref, wa_ref, wb_ref, gng_ref, gnb_ref,
                    lng_ref, lnb_ref, bd_ref, y_ref):
    bd = bd_ref[...]
    o = o_ref[...]
    inv = 1.0 / B_HEAD_DIM
    om = _head_sum(o, bd) * inv
    oc = o - om
    ov = _head_sum(oc * oc, bd) * inv
    ob = (oc * lax.rsqrt(ov + GN_EPS) * gng_ref[...] + gnb_ref[...] + bonus_ref[...]) * g_ref[...]
    mix = (jnp.dot(oa_ref[...].astype(BF16), wa_ref[...], preferred_element_type=F32)
           + jnp.dot(ob.astype(BF16), wb_ref[...], preferred_element_type=F32))
    y_ref[...] = _layer_norm(ALPHA * x_ref[...] + mix, lng_ref[...], lnb_ref[...])


def _outproj(oA, o, bonus, g, x, prm, tm):
    N = x.shape[0]
    tokA = pl.BlockSpec((tm, A_WIDTH), lambda i: (i, 0))
    tokD = pl.BlockSpec((tm, D_MODEL), lambda i: (i, 0))
    full = lambda a: pl.BlockSpec(a.shape, lambda i: (0, 0))
    return pl.pallas_call(
        _outproj_kernel,
        grid=(N // tm,),
        in_specs=[tokA, tokA, tokA, tokA, tokD, full(prm['wo_a']), full(prm['wo_b']),
                  full(prm['gn_g']), full(prm['gn_b']), full(prm['ln1_g']), full(prm['ln1_b']),
                  full(prm['bd'])],
        out_specs=tokD,
        out_shape=jax.ShapeDtypeStruct((N, D_MODEL), F32),
        compiler_params=_cparams(("parallel",)),
        name="out_proj_ln",
    )(oA, o, bonus, g, x, prm['wo_a'], prm['wo_b'], prm['gn_g'], prm['gn_b'],
      prm['ln1_g'], prm['ln1_b'], prm['bd'])


def _router_kernel(x_ref, rwh_ref, rwl_ref, rb_ref, tri_ref, gate_ref, esel_ref, rank_ref, cnt_ref, oh_scr):
    xh, xl = _split(x_ref[...])
    nt = lambda a, b: lax.dot_general(a, b, (((1,), (1,)), ((), ())), preferred_element_type=F32)
    lg = nt(rwh_ref[...], xh) + nt(rwh_ref[...], xl) + nt(rwl_ref[...], xh)
    sc = jax.nn.sigmoid(lg)
    bi = sc + rb_ref[...]
    G, M = N_GROUPS, EXPERTS_PER_GROUP
    X = [bi[m * G:(m + 1) * G] for m in range(M)]
    S = [sc[m * G:(m + 1) * G] for m in range(M)]
    gs = None
    for a in range(M):
        for b in range(a + 1, M):
            pr = X[a] + X[b]
            gs = pr if gs is None else jnp.maximum(gs, pr)
    best = gs[0:1]
    gsel = jnp.zeros_like(best, dtype=jnp.int32)
    for g in range(1, G):
        c = gs[g:g + 1] > best
        gsel = jnp.where(c, g, gsel)
        best = jnp.where(c, gs[g:g + 1], best)
    vals, raws = [], []
    for m in range(M):
        vm, sm = X[m][0:1], S[m][0:1]
        for g in range(1, G):
            vm = jnp.where(gsel == g, X[m][g:g + 1], vm)
            sm = jnp.where(gsel == g, S[m][g:g + 1], sm)
        vals.append(vm)
        raws.append(sm)

    def first_argmax(vs):
        bv, bi_ = vs[0], jnp.zeros_like(gsel)
        for m in range(1, M):
            c = vs[m] > bv
            bi_ = jnp.where(c, m, bi_)
            bv = jnp.where(c, vs[m], bv)
        return bi_

    i1 = first_argmax(vals)
    i2 = first_argmax([jnp.where(i1 == m, -jnp.inf, vals[m]) for m in range(M)])
    w1 = sum(jnp.where(i1 == m, raws[m], 0.0) for m in range(M))
    w2 = sum(jnp.where(i2 == m, raws[m], 0.0) for m in range(M))
    den = w1 + w2
    gate_ref[0:1, :] = w1 / den
    gate_ref[1:2, :] = w2 / den
    esel_ref[0:1, :] = gsel * M + i1
    esel_ref[1:2, :] = gsel * M + i2

    @pl.when(pl.program_id(0) == 0)
    def _():
        cnt_ref[...] = jnp.zeros(cnt_ref.shape, F32)

    for g in range(G):
        for m in range(M):
            hit = ((gsel == g) & (i1 == m)) | ((gsel == g) & (i2 == m))
            oh_scr[g * M + m:g * M + m + 1, :] = jnp.where(hit, 1.0, 0.0)
    oh = oh_scr[...]
    before = cnt_ref[...] + jnp.dot(oh.astype(BF16), tri_ref[...], preferred_element_type=F32)
    r1 = jnp.zeros_like(w1)
    r2 = jnp.zeros_like(w1)
    for g in range(G):
        for m in range(M):
            e = g * M + m
            r1 = jnp.where((gsel == g) & (i1 == m), before[e:e + 1], r1)
            r2 = jnp.where((gsel == g) & (i2 == m), before[e:e + 1], r2)
    rank_ref[0:1, :] = r1.astype(jnp.int32)
    rank_ref[1:2, :] = r2.astype(jnp.int32)
    cnt_ref[...] += jnp.sum(oh, axis=1, keepdims=True)


def _router(x, rw, rb, tm):
    N = x.shape[0]
    rwh, rwl = rw
    tri = jnp.asarray(np.arange(tm)[:, None] < np.arange(tm)[None, :], BF16)
    slot = pl.BlockSpec((2, tm), lambda i: (0, i))
    return pl.pallas_call(
        _router_kernel,
        grid=(N // tm,),
        in_specs=[pl.BlockSpec((tm, D_MODEL), lambda i: (i, 0)),
                  pl.BlockSpec(rwh.shape, lambda i: (0, 0)),
                  pl.BlockSpec(rwl.shape, lambda i: (0, 0)),
                  pl.BlockSpec(rb.shape, lambda i: (0, 0)),
                  pl.BlockSpec(tri.shape, lambda i: (0, 0))],
        out_specs=[slot, slot, slot, pl.BlockSpec((N_EXPERTS, 1), lambda i: (0, 0))],
        out_shape=[jax.ShapeDtypeStruct((2, N), F32), jax.ShapeDtypeStruct((2, N), jnp.int32),
                   jax.ShapeDtypeStruct((2, N), jnp.int32), jax.ShapeDtypeStruct((N_EXPERTS, 1), F32)],
        scratch_shapes=[pltpu.VMEM((N_EXPERTS, tm), F32)],
        compiler_params=_cparams(("arbitrary",)),
        name="router",
    )(x, rwh, rwl, rb, tri)


def _row_copy(src, src_row, dst, dst_row, sem):
    return pltpu.make_async_copy(src.at[pl.ds(src_row, 1)], dst.at[pl.ds(dst_row, 1)], sem)


def _dispatch_kernel(dest_ref, x_ref, xs_in_ref, xs_ref, sem, *, N, tb):
    del xs_in_ref
    base = pl.program_id(0) * tb

    def issue(t, c):
        for s in range(2):
            _row_copy(x_ref, t, xs_ref, dest_ref[s * N + base + t], sem.at[0]).start()
        return c

    lax.fori_loop(0, tb, issue, 0, unroll=8)

    def drain(t, c):
        for s in range(2):
            _row_copy(x_ref, 0, xs_ref, 0, sem.at[0]).wait()
        return c

    lax.fori_loop(0, tb, drain, 0, unroll=8)


def _dispatch(dest, x, rows, tb):
    N = x.shape[0]
    return pl.pallas_call(
        functools.partial(_dispatch_kernel, N=N, tb=tb),
        grid_spec=pltpu.PrefetchScalarGridSpec(
            num_scalar_prefetch=1,
            grid=(N // tb,),
            in_specs=[pl.BlockSpec((tb, D_MODEL), lambda i, d: (i, 0)),
                      pl.BlockSpec(memory_space=pl.ANY)],
            out_specs=pl.BlockSpec(memory_space=pl.ANY),
            scratch_shapes=[pltpu.SemaphoreType.DMA((1,))],
        ),
        out_shape=jax.ShapeDtypeStruct((rows, D_MODEL), F32),
        input_output_aliases={2: 0},
        compiler_params=_cparams(("arbitrary",)),
        name="moe_dispatch",
    )(dest, x, jnp.zeros((rows, D_MODEL), F32))


def _experts_kernel(te_ref, tv_ref, xs_ref, w1_ref, w3_ref, w2_ref, y_ref):
    del te_ref

    @pl.when(tv_ref[pl.program_id(0)] != 0)
    def _():
        xb = xs_ref[...].astype(BF16)
        h1 = jnp.dot(xb, w1_ref[0], preferred_element_type=F32)
        h3 = jnp.dot(xb, w3_ref[0], preferred_element_type=F32)
        hh = (h1 * jax.nn.sigmoid(h1) * h3).astype(BF16)
        y_ref[...] = jnp.dot(hh, w2_ref[0], preferred_element_type=F32)

    @pl.when(tv_ref[pl.program_id(0)] == 0)
    def _():
        y_ref[...] = jnp.zeros(y_ref.shape, F32)


def _experts(tile_expert, tile_valid, xs, w1, w3, w2, tm):
    rows = xs.shape[0]
    wspec = lambda shp: pl.BlockSpec((1,) + shp, lambda i, te, tv: (te[i], 0, 0))
    return pl.pallas_call(
        _experts_kernel,
        grid_spec=pltpu.PrefetchScalarGridSpec(
            num_scalar_prefetch=2,
            grid=(rows // tm,),
            in_specs=[pl.BlockSpec((tm, D_MODEL), lambda i, te, tv: (i, 0)),
                      wspec((D_MODEL, D_EXPERT)), wspec((D_MODEL, D_EXPERT)), wspec((D_EXPERT, D_MODEL))],
            out_specs=pl.BlockSpec((tm, D_MODEL), lambda i, te, tv: (i, 0)),
        ),
        out_shape=jax.ShapeDtypeStruct((rows, D_MODEL), F32),
        compiler_params=_cparams(("arbitrary",)),
        name="moe_experts",
    )(tile_expert, tile_valid, xs, w1, w3, w2)


def _combine_kernel(dest_ref, x_ref, gate_ref, ys_ref, lng_ref, lnb_ref, o_ref, ybuf, sem, *, N, tb):
    base = pl.program_id(0) * tb

    def issue(t, c):
        for s in range(2):
            _row_copy(ys_ref, dest_ref[s * N + base + t], ybuf.at[s], t, sem.at[0]).start()
        return c

    lax.fori_loop(0, tb, issue, 0, unroll=8)

    def drain(t, c):
        for s in range(2):
            _row_copy(ys_ref, 0, ybuf.at[s], 0, sem.at[0]).wait()
        return c

    lax.fori_loop(0, tb, drain, 0, unroll=8)
    gate = gate_ref[...]
    f = gate[:, 0:1] * ybuf[0] + gate[:, 1:2] * ybuf[1]
    o_ref[...] = _layer_norm(ALPHA * x_ref[...] + f, lng_ref[...], lnb_ref[...])


def _combine(dest, x, gate_t, ys, lng, lnb, tb):
    N = x.shape[0]
    full = lambda a: pl.BlockSpec(a.shape, lambda i, d: (0, 0))
    return pl.pallas_call(
        functools.partial(_combine_kernel, N=N, tb=tb),
        grid_spec=pltpu.PrefetchScalarGridSpec(
            num_scalar_prefetch=1,
            grid=(N // tb,),
            in_specs=[pl.BlockSpec((tb, D_MODEL), lambda i, d: (i, 0)),
                      pl.BlockSpec((tb, 2), lambda i, d: (i, 0)),
                      pl.BlockSpec(memory_space=pl.ANY), full(lng), full(lnb)],
            out_specs=pl.BlockSpec((tb, D_MODEL), lambda i, d: (i, 0)),
            scratch_shapes=[pltpu.VMEM((2, tb, D_MODEL), F32), pltpu.SemaphoreType.DMA((1,))],
        ),
        out_shape=jax.ShapeDtypeStruct((N, D_MODEL), F32),
        compiler_params=_cparams(("arbitrary",)),
        name="moe_combine_ln",
    )(dest, x, gate_t, ys, lng, lnb)


def _moe(x, rw, rb, w1, w3, w2, lng, lnb):
    N = x.shape[0]
    tm = min(512, N)
    gate, esel, rank, counts = _router(x, rw, rb, tm)
    n_tiles = 2 * N // tm + N_EXPERTS
    cnt = counts[:, 0].astype(jnp.int32)
    padded = (cnt + tm - 1) // tm * tm
    ends = jnp.cumsum(padded)
    starts = ends - padded
    offset = sum(jnp.where(esel == e, starts[e], 0) for e in range(N_EXPERTS))
    dest = (offset + rank).reshape(-1)
    tile_start = jnp.arange(n_tiles, dtype=jnp.int32) * tm
    tile_valid = (tile_start < ends[-1]).astype(jnp.int32)
    last_start = jnp.maximum(ends[-1] - tm, 0)
    tile_expert = jnp.sum(jnp.minimum(tile_start, last_start)[:, None] >= ends[None, :], axis=1).astype(jnp.int32)
    xs = _dispatch(dest, x, n_tiles * tm, min(256, N))
    ys = _experts(tile_expert, tile_valid, xs, w1, w3, w2, tm)
    return _combine(dest, x, gate.T, ys, lng, lnb, min(256, N))


def _row(v, n=None):
    v = v.reshape(1, -1).astype(F32)
    if n is not None and v.shape[1] < n:
        v = jnp.pad(v, ((0, 0), (0, n - v.shape[1])))
    return v


def _layer_params(l, w_in, w_out, ln1_g, ln1_b, ln2_g, ln2_b, rwkv_mu, rwkv_w0, rwkv_w2, rwkv_a0,
                  rwkv_a2, rwkv_g2, rwkv_kk, rwkv_ka, rwkv_rk, rwkv_gn_g, rwkv_gn_b,
                  exp_w1, exp_w3, exp_w2):
    wi = w_in[l]
    offs = np.cumsum((0, A_WIDTH, A_KV_WIDTH, A_KV_WIDTH, IDX_WIDTH, IDX_DIM, IDX_HEADS))
    q, k, v, qi, ki, wgt = (wi[:, offs[n]:offs[n + 1]] for n in range(6))
    w_a1 = jnp.concatenate([q, k, v], axis=1)
    w_a2 = jnp.concatenate([qi, ki, wgt, jnp.zeros((D_MODEL, LANES - IDX_DIM - IDX_HEADS), F32)], axis=1)
    w_a2h, w_a2l = _split(w_a2)
    w_b = jnp.pad(wi[:, A_PROJ:], ((0, 0), (0, B_PAD - B_PROJ)))
    zl = jnp.zeros((DECAY_LORA, B_WIDTH), F32)
    seg = np.arange(LANES) // B_HEAD_DIM
    return dict(
        w_a1=w_a1.astype(BF16), w_a2h=w_a2h, w_a2l=w_a2l, w_b=w_b.astype(BF16),
        wo_a=w_out[l][:A_WIDTH].astype(BF16), wo_b=w_out[l][A_WIDTH:].astype(BF16),
        ln1_g=_row(ln1_g[l]), ln1_b=_row(ln1_b[l]), ln2_g=_row(ln2_g[l]), ln2_b=_row(ln2_b[l]),
        mu=_row(rwkv_mu[l], B_PAD), w0=_row(rwkv_w0[l]), a0=_row(rwkv_a0[l]),
        ww=jnp.concatenate([rwkv_w2[l], zl], axis=0).astype(BF16),
        wa=jnp.concatenate([zl, rwkv_a2[l]], axis=0).astype(BF16),
        wg=jnp.pad(rwkv_g2[l], ((0, 2 * LANES - GATE_LORA), (0, 0))).astype(BF16),
        kk=_row(rwkv_kk[l]), ka=_row(rwkv_ka[l]), rk=_row(rwkv_rk[l]),
        gn_g=_row(rwkv_gn_g[l]), gn_b=_row(rwkv_gn_b[l]),
        bd=jnp.asarray(seg[:, None] == seg[None, :], BF16),
        w1=exp_w1[l].astype(BF16), w3=exp_w3[l].astype(BF16), w2=exp_w2[l].astype(BF16),
    )


def _trunk_layer(x, prm, rw, rb, past):
    B, T, _ = x.shape
    N = B * T
    xt = x.reshape(N, D_MODEL)
    tm = min(512, N)
    hA1 = _matmul(xt, prm['w_a1'], tm, A1_PAD // 2).reshape(B, T, A1_PAD)
    hA2 = _matmul(xt, prm['w_a2h'], tm, A2_PAD // 3, prm['w_a2l']).reshape(B, T, A2_PAD)
    hB = _matmul(xt, prm['w_b'], tm, B_PAD // 3).reshape(B, T, B_PAD)
    k_new = hA1[..., A1_OFF_K:A1_OFF_K + A_KV_WIDTH]
    v_new = hA1[..., A1_OFF_V:A1_OFF_V + A_KV_WIDTH]
    ki_new = hA2[..., A2_OFF_KW:A2_OFF_KW + IDX_DIM]

    if past is None:
        kf, vf, kif = k_new, v_new, ki_new
        pos0, TQ = 0, min(Q_BLOCK, T)
        S0 = jnp.zeros((B, B_HEADS, B_HEAD_DIM, B_HEAD_DIM), F32)
        prev = jnp.zeros((B, 1, B_PAD), F32)
    else:
        ck, cv, cki, S0, prev = past
        P = ck.shape[1]
        kf = jnp.concatenate([ck.reshape(B, P, A_KV_WIDTH), k_new], axis=1)
        vf = jnp.concatenate([cv.reshape(B, P, A_KV_WIDTH), v_new], axis=1)
        kif = jnp.concatenate([cki, ki_new], axis=1)
        pos0, TQ = P, T
        prev = jnp.pad(prev, ((0, 0), (0, 0), (0, B_PAD - B_PROJ)))
    L = kf.shape[1]
    ksel = min(TOPK_MAX, L // 4)
    TK = min(512, -(-L // LANES) * LANES)
    Lp = -(-L // TK) * TK
    padk = lambda a: jnp.pad(a.astype(BF16), ((0, 0), (0, Lp - L), (0, 0)))
    ki_hi, ki_lo = _split(kif.astype(F32))
    ki3 = jnp.concatenate([ki_hi, ki_hi, ki_lo, jnp.zeros_like(ki_hi)], axis=-1)
    oA = _dsa(hA1, hA2, padk(kf), padk(vf), padk(ki3), TQ=TQ, L=L, pos0=pos0, ksel=ksel)

    r, d, k, v, q, b, g, bonus = _rwkv_prep(hB, prev, prm, min(256, T))
    o_s, hT = _rwkv_scan(_to_scan_j(q, B, T), _to_scan_j(d, B, T), _to_scan_j(b, B, T),
                         _to_scan_j(k, B, T), _to_scan_j(r, B, T), _to_scan_i(v, B, T),
                         _state_to_scan(S0.astype(F32), B), min(32, T))
    o = _from_scan_i(o_s, B, T)
    S_T = _state_from_scan(hT, B)
    last = hB[:, -1:, :B_PROJ]

    tmo = min(256, N)
    x1 = _outproj(oA.reshape(N, A_WIDTH), o.reshape(N, B_WIDTH), bonus.reshape(N, B_WIDTH),
                  g.reshape(N, B_WIDTH), xt, prm, tmo)
    x2 = _moe(x1, rw, rb, prm['w1'], prm['w3'], prm['w2'], prm['ln2_g'], prm['ln2_b'])
    state = (k_new.reshape(B, T, A_KV_HEADS, A_HEAD_DIM), v_new.reshape(B, T, A_KV_HEADS, A_HEAD_DIM),
             ki_new, S_T, last)
    return x2.reshape(B, T, D_MODEL), state


def kernel(x_prompt, x_sample, cache_k, cache_v, cache_kidx, state_wkv, state_shift, w_in, w_out, ln1_g, ln1_b, ln2_g, ln2_b, rwkv_mu, rwkv_w0, rwkv_w2, rwkv_a0, rwkv_a2, rwkv_g2, rwkv_kk, rwkv_ka, rwkv_rk, rwkv_gn_g, rwkv_gn_b, router_w, router_b, exp_w1, exp_w3, exp_w2):
    assert x_prompt.shape[0] % 4 == 0 and x_sample.shape[0] % 4 == 0
    perm = np.array([g * EXPERTS_PER_GROUP + m for m in range(EXPERTS_PER_GROUP) for g in range(N_GROUPS)])
    rw = _split(router_w.T[perm].astype(F32))
    rb = router_b[perm].reshape(N_EXPERTS, 1).astype(F32)
    yp, ys = x_prompt, x_sample
    new_p, new_s = [], []
    for l in range(DEPTH):
        prm = _layer_params(l, w_in, w_out, ln1_g, ln1_b, ln2_g, ln2_b, rwkv_mu, rwkv_w0, rwkv_w2,
                            rwkv_a0, rwkv_a2, rwkv_g2, rwkv_kk, rwkv_ka, rwkv_rk, rwkv_gn_g, rwkv_gn_b,
                            exp_w1, exp_w3, exp_w2)
        yp, st_p = _trunk_layer(yp, prm, rw, rb, None)
        ys, st_s = _trunk_layer(ys, prm, rw, rb, (cache_k[l], cache_v[l], cache_kidx[l],
                                                   state_wkv[l], state_shift[l]))
        new_p.append(st_p)
        new_s.append(st_s)
    stack = lambda sts, n: jnp.stack([s[n] for s in sts])
    return (yp, ys) + tuple(stack(new_p, n) for n in range(5)) + tuple(stack(new_s, n) for n in range(5))
```
